```python
import jax
import jax.numpy as jnp
from jax import lax
import numpy as np

D_MODEL = 1024
BATCH = 2
SEQ = 16384
DEPTH = 4

GRID_W = 64
CTX_LEN = 256
HEAD_DIM = 64
NA_HEADS = 6
NA_DIM = NA_HEADS * HEAD_DIM
NA_ROWS = 8
NA_COLS = 16
NA_QBLOCK = 16
NA_KSPAN = 32
HG_HEADS = 6
HG_DK = 64
HG_DV = 64
HG_DIM = HG_HEADS * HG_DV
HG_CHUNK = 64
CONV_DIM = 256
CONV_WIDTH = 3
MIX_DIM = NA_DIM + HG_DIM + CONV_DIM
N_EXPERTS = 16
EXPERT_DIM = 512
EC_CAPACITY_FACTOR = 2
ROPE_BASE = 10000.0
NORM_EPS = 1e-6
N_MOD = 6

kernel_name = 'hybrid_na_hgrn2_shortconv_ecmoe_dit'


def _in_sizes():
    return [NA_DIM] * 3 + [HG_HEADS * HG_DK] * 3 + [HG_HEADS * HG_DV] * 2 + [CONV_DIM] * 3


def _split_points():
    return [int(s) for s in np.cumsum(_in_sizes())[:-1]]


def _rmsnorm(x, w):
    xf = x.astype(jnp.float32)
    y = xf * lax.rsqrt(jnp.mean(xf * xf, axis=-1, keepdims=True) + NORM_EPS)
    return (y * w.astype(jnp.float32)).astype(x.dtype)


def _adaln(cond, w, b):
    return jnp.matmul(jax.nn.silu(cond), w) + b


def _modulate(h, shift, scale):
    return h * (1 + scale) + shift


def _rope_tables(n, dtype):
    t = jnp.arange(n, dtype=jnp.int32)
    row = (t // GRID_W).astype(jnp.float32)
    col = (t % GRID_W).astype(jnp.float32)
    n_freq = HEAD_DIM // 4
    inv = ROPE_BASE ** (-jnp.arange(n_freq, dtype=jnp.float32) / n_freq)
    ar = row[:, None] * inv
    ac = col[:, None] * inv
    ang = jnp.concatenate([ar, ar, ac, ac], axis=-1)
    return jnp.cos(ang).astype(dtype)[:, None, :], jnp.sin(ang).astype(dtype)[:, None, :]


def _rot(x):
    x0, x1, x2, x3 = jnp.split(x, 4, axis=-1)
    return jnp.concatenate([-x1, x0, -x3, x2], axis=-1)


def _rope(x, cos, sin):
    return x * cos + _rot(x) * sin


def _na_col_tables():
    cols = np.arange(GRID_W)
    win_start = np.clip(cols - NA_COLS // 2, 0, GRID_W - NA_COLS)
    n_cb = GRID_W // NA_QBLOCK
    blk_start = np.clip(win_start[::NA_QBLOCK], 0, GRID_W - NA_KSPAN)
    key_cols = blk_start[:, None] + np.arange(NA_KSPAN)[None, :]
    q_cols = cols.reshape(n_cb, NA_QBLOCK)
    q_start = win_start.reshape(n_cb, NA_QBLOCK)
    kc = key_cols[:, None, :]
    mask = (kc >= q_start[..., None]) & (kc < q_start[..., None] + NA_COLS)
    rel = np.clip(kc - q_cols[..., None] + NA_COLS - 1, 0, 2 * NA_COLS - 2)
    return key_cols, rel, mask


def _na_latent(q, k, v, k_ctx, v_ctx, rpb):
    b, n, h, d = q.shape
    rows = n // GRID_W
    kr = min(NA_ROWS, rows)
    n_cb = GRID_W // NA_QBLOCK
    key_cols_np, rel_col_np, col_mask_np = _na_col_tables()
    key_cols = jnp.asarray(key_cols_np)
    rel_col = jnp.asarray(rel_col_np)
    n_loc = kr * NA_KSPAN
    win_mask = jnp.asarray(np.broadcast_to(col_mask_np[:, :, None, :], (n_cb, NA_QBLOCK, kr, NA_KSPAN)).reshape(n_cb, NA_QBLOCK, n_loc))
    kg = k.reshape(b, rows, GRID_W, h, d)
    vg = v.reshape(b, rows, GRID_W, h, d)
    q_rows = q.reshape(b, rows, n_cb, NA_QBLOCK, h, d).transpose(1, 0, 2, 3, 4, 5)
    scale = d ** -0.5

    def gather_blocks(g, rs):
        blk = lax.dynamic_slice_in_dim(g, rs, kr, axis=1)[:, :, key_cols]
        return blk.transpose(0, 2, 1, 3, 4, 5).reshape(b, n_cb, n_loc, h, d)

    def row_fn(xs):
        r, q_r = xs
        rs = jnp.clip(r - kr // 2, 0, rows - kr)
        k_loc = gather_blocks(kg, rs)
        v_loc = gather_blocks(vg, rs)
        rel_row = rs + jnp.arange(kr) - r + (NA_ROWS - 1)
        bias = rpb[:, rel_row[:, None, None, None], rel_col[None]]
        bias = bias.transpose(0, 2, 3, 1, 4).reshape(h, n_cb, NA_QBLOCK, n_loc)
        s_loc = jnp.einsum('bjqhd,bjkhd->bhjqk', q_r, k_loc).astype(jnp.float32) * scale + bias.astype(jnp.float32)
        s_loc = jnp.where(win_mask, s_loc, -jnp.inf)
        s_ctx = jnp.einsum('bjqhd,bchd->bhjqc', q_r, k_ctx).astype(jnp.float32) * scale
        p = jax.nn.softmax(jnp.concatenate([s_loc, s_ctx], axis=-1), axis=-1).astype(v.dtype)
        o_loc = jnp.einsum('bhjqk,bjkhd->bjqhd', p[..., :n_loc], v_loc)
        o_ctx = jnp.einsum('bhjqc,bchd->bjqhd', p[..., n_loc:], v_ctx)
        return o_loc + o_ctx

    out = lax.map(row_fn, (jnp.arange(rows, dtype=jnp.int32), q_rows))
    return out.transpose(1, 0, 2, 3, 4, 5).reshape(b, n, h, d)


def _ctx_attn(q, k, v):
    s = jnp.einsum('bqhd,bkhd->bhqk', q, k).astype(jnp.float32) * (q.shape[-1] ** -0.5)
    p = jax.nn.softmax(s, axis=-1).astype(v.dtype)
    return jnp.einsum('bhqk,bkhd->bqhd', p, v)


def _hgrn_lower_bounds(logits):
    p = jax.nn.softmax(logits.astype(jnp.float32), axis=1)
    return jnp.concatenate([jnp.zeros_like(p[:, :1]), jnp.cumsum(p[:, 1:], axis=1)], axis=1)


def _hgrn_gates(f_logit, lb):
    z = f_logit.astype(jnp.float32)
    log_f = jnp.logaddexp(jnp.log(lb), jnp.log1p(-lb) + jax.nn.log_sigmoid(z))
    k = (1 - lb) * jax.nn.sigmoid(-z)
    return log_f, k


def _heads(a, dh):
    return a.reshape(a.shape[0], a.shape[1], -1, dh).astype(jnp.float32)


def _gla_scan(q, k, log_f, v, s0):
    b, t, h, _ = q.shape
    nc = t // HG_CHUNK

    def to_chunks(a):
        return a.reshape(b, nc, HG_CHUNK, h, a.shape[-1]).transpose(1, 0, 3, 2, 4)

    causal = jnp.tril(jnp.ones((HG_CHUNK, HG_CHUNK), dtype=bool))

    def step(s, xs):
        qc, kc, gc, vc = xs
        a = jnp.cumsum(gc, axis=2)
        diff = a[:, :, :, None, :] - a[:, :, None, :, :]
        decay = jnp.exp(jnp.where(causal[:, :, None], diff, -jnp.inf))
        scores = jnp.sum(qc[:, :, :, None, :] * kc[:, :, None, :, :] * decay, axis=-1)
        o = jnp.einsum('bhts,bhsv->bhtv', scores, vc) + jnp.einsum('bhtk,bhkv->bhtv', qc * jnp.exp(a), s)
        a_last = a[:, :, -1:, :]
        s_new = jnp.exp(a_last[:, :, 0, :, None]) * s + jnp.einsum('bhsk,bhsv->bhkv', kc * jnp.exp(a_last - a), vc)
        return s_new, o

    s_fin, o = lax.scan(step, s0, (to_chunks(q), to_chunks(k), to_chunks(log_f), to_chunks(v)))
    o = o.transpose(1, 0, 3, 2, 4).reshape(b, t, h, v.shape[-1])
    return o, s_fin


def _hgrn2_mixer(lat, cxt, lb_fwd, lb_bwd, norm_w, need_ctx):
    q_l, ff_l, fb_l, i_l, g_l = lat
    q_c, ff_c, fb_c, i_c, g_c = cxt
    dtype = q_l.dtype
    b = q_l.shape[0]
    scale = HG_DK ** -0.5
    ql = _heads(q_l, HG_DK) * scale
    qc = _heads(q_c, HG_DK) * scale
    vl = _heads(i_l, HG_DV)
    vc = _heads(i_c, HG_DV)
    o_l = jnp.zeros_like(vl)
    o_c = jnp.zeros_like(vc)
    for f_lat, f_ctx, lb, rev in ((ff_l, ff_c, lb_fwd, False), (fb_l, fb_c, lb_bwd, True)):
        lb = lb.astype(jnp.float32).reshape(HG_HEADS, HG_DK)
        lf_l, k_l = _hgrn_gates(_heads(f_lat, HG_DK), lb)
        lf_c, k_c = _hgrn_gates(_heads(f_ctx, HG_DK), lb)
        lat_seq = (ql, k_l, lf_l, vl)
        ctx_seq = (qc, k_c, lf_c, vc)
        if rev:
            lat_seq = tuple(jnp.flip(a, axis=1) for a in lat_seq)
            ctx_seq = tuple(jnp.flip(a, axis=1) for a in ctx_seq)
        s0 = jnp.zeros((b, HG_HEADS, HG_DK, HG_DV), jnp.float32)
        oc, s_ctx = _gla_scan(*ctx_seq, s0)
        ol, _ = _gla_scan(*lat_seq, s_ctx)
        if rev:
            oc = jnp.flip(oc, axis=1)
            ol = jnp.flip(ol, axis=1)
        o_l = o_l + ol
        o_c = o_c + oc
    y_l = (_rmsnorm(o_l, norm_w) * jax.nn.silu(_heads(g_l, HG_DV))).reshape(g_l.shape).astype(dtype)
    if not need_ctx:
        return y_l, None
    y_c = (_rmsnorm(o_c, norm_w) * jax.nn.silu(_heads(g_c, HG_DV))).reshape(g_c.shape).astype(dtype)
    return y_l, y_c


def _short_gated_conv(b_gate, c_gate, u, w):
    z = c_gate * u
    t = z.shape[1]
    zp = jnp.pad(z, ((0, 0), (CONV_WIDTH // 2, CONV_WIDTH // 2), (0, 0)))
    y = sum(w[j] * zp[:, j:j + t] for j in range(CONV_WIDTH))
    return b_gate * y


def _token_mixers(h, hc, w_in_l, q_norm, k_norm, rpb, lb_fwd, lb_bwd, hg_norm_l, conv_w_l, rope_cos, rope_sin, need_ctx):
    b, n, _ = h.shape
    ctx_len = hc.shape[1]
    pts = _split_points()
    p = jnp.split(jnp.einsum('btd,dk->btk', h, w_in_l), pts, axis=-1)
    pc = jnp.split(jnp.einsum('btd,dk->btk', hc, w_in_l), pts, axis=-1)

    def na_heads(a):
        return a.reshape(a.shape[0], a.shape[1], NA_HEADS, HEAD_DIM)

    q = _rope(_rmsnorm(na_heads(p[0]), q_norm), rope_cos, rope_sin)
    k = _rope(_rmsnorm(na_heads(p[1]), k_norm), rope_cos, rope_sin)
    v = na_heads(p[2])
    qc = _rmsnorm(na_heads(pc[0]), q_norm)
    kc = _rmsnorm(na_heads(pc[1]), k_norm)
    vc = na_heads(pc[2])
    na = _na_latent(q, k, v, kc, vc, rpb).reshape(b, n, NA_DIM)
    hg, hg_c = _hgrn2_mixer(p[3:8], pc[3:8], lb_fwd, lb_bwd, hg_norm_l, need_ctx)
    cv = _short_gated_conv(p[8], p[9], p[10], conv_w_l)
    mix = jnp.concatenate([na, hg, cv], axis=-1)
    if not need_ctx:
        return mix, None
    na_c = _ctx_attn(qc, kc, vc).reshape(b, ctx_len, NA_DIM)
    cv_c = _short_gated_conv(pc[8], pc[9], pc[10], conv_w_l)
    return mix, jnp.concatenate([na_c, hg_c, cv_c], axis=-1)


def _expert_choice_ffn(h, w_router, w_gate, w_up, w_down):
    b, t, d = h.shape
    cap = EC_CAPACITY_FACTOR * t // N_EXPERTS
    aff = jax.nn.softmax(jnp.einsum('btd,de->bte', h, w_router).astype(jnp.float32), axis=-1)
    vals, idx = lax.top_k(aff.transpose(0, 2, 1), cap)
    xg = jax.vmap(lambda hb, ib: hb[ib])(h, idx)
    hid = jax.nn.silu(jnp.einsum('becd,edf->becf', xg, w_gate)) * jnp.einsum('becd,edf->becf', xg, w_up)
    out = jnp.einsum('becf,efd->becd', hid, w_down) * vals[..., None].astype(h.dtype)
    return jax.vmap(lambda ob, ib: jnp.zeros((t, d), h.dtype).at[ib.reshape(-1)].add(ob.reshape(-1, d)))(out, idx)


def setup_inputs(seed: int = 0) -> dict:
    key = jax.random.key(seed)
    ks = jax.random.split(key, 20)
    nrm = jax.random.normal
    d = D_MODEL
    in_cols = sum(_in_sizes())
    return {
        'x': nrm(ks[0], (BATCH, SEQ, d), jnp.float32),
        'c': nrm(ks[1], (BATCH, d), jnp.float32),
        'ctx': nrm(ks[2], (BATCH, CTX_LEN, d), jnp.float32),
        'c_ctx': nrm(ks[3], (d,), jnp.float32),
        'w_mod': nrm(ks[4], (DEPTH, d, N_MOD * d), jnp.float32) * (0.5 * d ** -0.5),
        'b_mod': 0.02 * nrm(ks[5], (DEPTH, N_MOD * d), jnp.float32),
        'norm1_w': 1.0 + 0.1 * nrm(ks[6], (DEPTH, d), jnp.float32),
        'w_in': nrm(ks[7], (DEPTH, d, in_cols), jnp.float32) * d ** -0.5,
        'na_q_norm': 1.0 + 0.1 * nrm(ks[8], (DEPTH, HEAD_DIM), jnp.float32),
        'na_k_norm': 1.0 + 0.1 * nrm(ks[9], (DEPTH, HEAD_DIM), jnp.float32),
        'na_rpb': 0.5 * nrm(ks[10], (DEPTH, NA_HEADS, 2 * NA_ROWS - 1, 2 * NA_COLS - 1), jnp.float32),
        'hg_lb_logits': nrm(ks[11], (2, DEPTH, HG_HEADS * HG_DK), jnp.float32),
        'hg_norm': 1.0 + 0.1 * nrm(ks[12], (DEPTH, HG_DV), jnp.float32),
        'conv_w': nrm(ks[13], (DEPTH, CONV_WIDTH, CONV_DIM), jnp.float32) * CONV_WIDTH ** -0.5,
        'w_out': nrm(ks[14], (DEPTH, MIX_DIM, d), jnp.float32) * MIX_DIM ** -0.5,
        'norm2_w': 1.0 + 0.1 * nrm(ks[15], (DEPTH, d), jnp.float32),
        'w_router': nrm(ks[16], (DEPTH, d, N_EXPERTS), jnp.float32) * d ** -0.5,
        'w_exp_gate': nrm(ks[17], (DEPTH, N_EXPERTS, d, EXPERT_DIM), jnp.float32) * d ** -0.5,
        'w_exp_up': nrm(ks[18], (DEPTH, N_EXPERTS, d, EXPERT_DIM), jnp.float32) * d ** -0.5,
        'w_exp_down': nrm(ks[19], (DEPTH, N_EXPERTS, EXPERT_DIM, d), jnp.float32) * EXPERT_DIM ** -0.5,
    }


def reference(x, c, ctx, c_ctx, w_mod, b_mod, norm1_w, w_in, na_q_norm, na_k_norm, na_rpb, hg_lb_logits, hg_norm, conv_w, w_out, norm2_w, w_router, w_exp_gate, w_exp_up, w_exp_down):
    n = x.shape[1]
    rope_cos, rope_sin = _rope_tables(n, x.dtype)
    lb = _hgrn_lower_bounds(hg_lb_logits)
    cx = ctx
    for l in range(DEPTH):
        need_ctx = l < DEPTH - 1
        m_lat = jnp.split(_adaln(c, w_mod[l], b_mod[l])[:, None, :], N_MOD, axis=-1)
        m_ctx = jnp.split(_adaln(c_ctx, w_mod[l], b_mod[l]), N_MOD, axis=-1)
        h = _modulate(_rmsnorm(x, norm1_w[l]), m_lat[0], m_lat[1])
        hc = _modulate(_rmsnorm(cx, norm1_w[l]), m_ctx[0], m_ctx[1])
        mix, mix_c = _token_mixers(h, hc, w_in[l], na_q_norm[l], na_k_norm[l], na_rpb[l], lb[0, l], lb[1, l], hg_norm[l], conv_w[l], rope_cos, rope_sin, need_ctx)
        x = x + m_lat[2] * jnp.einsum('btm,md->btd', mix, w_out[l])
        h2 = _modulate(_rmsnorm(x, norm2_w[l]), m_lat[3], m_lat[4])
        x = x + m_lat[5] * _expert_choice_ffn(h2, w_router[l], w_exp_gate[l], w_exp_up[l], w_exp_down[l])
        if need_ctx:
            cx = cx + m_ctx[2] * jnp.einsum('btm,md->btd', mix_c, w_out[l])
            hc2 = _modulate(_rmsnorm(cx, norm2_w[l]), m_ctx[3], m_ctx[4])
            cx = cx + m_ctx[5] * _expert_choice_ffn(hc2, w_router[l], w_exp_gate[l], w_exp_up[l], w_exp_down[l])
    return x
```

```python
import functools

import jax
import jax.numpy as jnp
import numpy as np
from jax import lax
from jax.experimental import pallas as pl
from jax.experimental.pallas import tpu as pltpu

GRID_W = 64
HEAD_DIM = 64
NA_HEADS = 6
NA_DIM = NA_HEADS * HEAD_DIM
NA_ROWS = 8
NA_COLS = 16
HG_HEADS = 6
HG_DIM = HG_HEADS * HEAD_DIM
HG_CHUNK = 64
HG_SUB = 16
CONV_DIM = 256
N_EXPERTS = 16
EC_CAPACITY_FACTOR = 2
ROPE_BASE = 10000.0
NORM_EPS = 1e-6
N_MOD = 6

LANES = 128
NA_QROWS = 4
NA_KROWS = NA_QROWS + NA_ROWS - 1
VMEM_LIMIT = 56 * 1024 * 1024

F32 = jnp.float32
BF16 = jnp.bfloat16
NEG_INF = float("-inf")


def _cparams(sem):
    return pltpu.CompilerParams(dimension_semantics=sem, vmem_limit_bytes=VMEM_LIMIT)


def _dot(a, b):
    return jnp.dot(a, b, preferred_element_type=F32)


def _dot_nt(a, b):
    return lax.dot_general(a, b, (((1,), (1,)), ((), ())), preferred_element_type=F32)


def _split3(x):
    hi = x.astype(BF16)
    r1 = x - hi.astype(F32)
    mid = r1.astype(BF16)
    lo = (r1 - mid.astype(F32)).astype(BF16)
    return hi, mid, lo


def _dot_sel_rhs(sel, x):
    hi, mid, lo = _split3(x)
    return _dot(sel, hi) + _dot(sel, mid) + _dot(sel, lo)


def _dot_sel_lhs(x, sel):
    hi, mid, lo = _split3(x)
    return _dot(hi, sel) + _dot(mid, sel) + _dot(lo, sel)


def _head_ones():
    r = lax.broadcasted_iota(jnp.int32, (LANES, LANES), 0) // HEAD_DIM
    c = lax.broadcasted_iota(jnp.int32, (LANES, LANES), 1) // HEAD_DIM
    return (r == c).astype(BF16)


def _silu(x):
    return x * (1.0 / (1.0 + jnp.exp(-x)))


def _modvec_kernel(cond_ref, w_ref, b_ref, o_ref):
    a = _silu(cond_ref[...]).astype(BF16)
    o_ref[0] = _dot(a, w_ref[0].astype(BF16)) + b_ref[0]


def _modvec(cond, w_mod, b_mod):
    depth, d, n = w_mod.shape
    tn = 512
    return pl.pallas_call(
        _modvec_kernel,
        grid=(depth, n // tn),
        in_specs=[
            pl.BlockSpec((8, d), lambda l, j: (0, 0)),
            pl.BlockSpec((1, d, tn), lambda l, j: (l, 0, j)),
            pl.BlockSpec((1, 1, tn), lambda l, j: (l, 0, j)),
        ],
        out_specs=pl.BlockSpec((1, 8, tn), lambda l, j: (l, 0, j)),
        out_shape=jax.ShapeDtypeStruct((depth, 8, n), F32),
        compiler_params=_cparams(("arbitrary", "arbitrary")),
    )(cond, w_mod, b_mod.reshape(depth, 1, n))


QKV_COLS = 3 * NA_DIM
REST_COLS = 3 * CONV_DIM + 5 * HG_DIM


def _rms_rows(x, w):
    ms = jnp.mean(x * x, axis=-1, keepdims=True)
    return x * lax.rsqrt(ms + NORM_EPS) * w


def _inproj_kernel(has_y, *refs):
    if has_y:
        (x_ref, y_ref, g_ref, sh_ref, sc_ref, nw_ref, w_ref, cos_ref, sa_ref, sb_ref, qn_ref, kn_ref,
         xo_ref, qkv_ref, rest_ref) = refs
        x = x_ref[0] + g_ref[0] * y_ref[0]
        xo_ref[0] = x
    else:
        (x_ref, sh_ref, sc_ref, nw_ref, w_ref, cos_ref, sa_ref, sb_ref, qn_ref, kn_ref,
         qkv_ref, rest_ref) = refs
        x = x_ref[0]
    h = _rms_rows(x, nw_ref[...]) * (1.0 + sc_ref[0]) + sh_ref[0]
    hb = h.astype(BF16)
    ones = _head_ones()
    cos = cos_ref[...]
    sa = sa_ref[...]
    sb = sb_ref[...]
    for c in range(2 * NA_DIM // LANES):
        p = _dot(hb, w_ref[:, c * LANES:(c + 1) * LANES])
        ssq = _dot_sel_lhs(p * p, ones)
        nw = qn_ref[...] if c < NA_DIM // LANES else kn_ref[...]
        pn = p * lax.rsqrt(ssq * (1.0 / HEAD_DIM) + NORM_EPS) * nw
        pr = pn * cos + pltpu.roll(pn, LANES - 16, 1) * sa + pltpu.roll(pn, 16, 1) * sb
        if c < NA_DIM // LANES:
            pr = pr * (HEAD_DIM ** -0.5)
        qkv_ref[0, :, c * LANES:(c + 1) * LANES] = pr.astype(BF16)
    qkv_ref[0, :, 2 * NA_DIM:] = _dot(hb, w_ref[:, 2 * NA_DIM:QKV_COLS]).astype(BF16)
    for c in range(REST_COLS // HG_DIM):
        lo = QKV_COLS + c * HG_DIM
        rest_ref[0, :, c * HG_DIM:(c + 1) * HG_DIM] = _dot(hb, w_ref[:, lo:lo + HG_DIM])


def _inproj(x, y, gate, shift, scale, norm_w, w_in, cos, sa, sb, qn, kn, tm):
    b, t, d = x.shape
    has_y = y is not None
    row = pl.BlockSpec((1, tm, d), lambda bi, i: (bi, i, 0))
    vec = pl.BlockSpec((1, 1, d), lambda bi, i: (bi, 0, 0))
    tab = pl.BlockSpec((tm, LANES), lambda bi, i: (i, 0))
    small = pl.BlockSpec((1, LANES), lambda bi, i: (0, 0))
    in_specs = [row] + ([row, vec] if has_y else []) + [
        vec, vec, pl.BlockSpec((1, d), lambda bi, i: (0, 0)),
        pl.BlockSpec(w_in.shape, lambda bi, i: (0, 0)), tab, tab, tab, small, small]
    out_specs = ([row] if has_y else []) + [
        pl.BlockSpec((1, tm, QKV_COLS), lambda bi, i: (bi, i, 0)),
        pl.BlockSpec((1, tm, REST_COLS), lambda bi, i: (bi, i, 0))]
    out_shape = ([jax.ShapeDtypeStruct((b, t, d), F32)] if has_y else []) + [
        jax.ShapeDtypeStruct((b, t, QKV_COLS), BF16), jax.ShapeDtypeStruct((b, t, REST_COLS), F32)]
    args = [x] + ([y, gate] if has_y else []) + [shift, scale, norm_w, w_in, cos, sa, sb, qn, kn]
    outs = pl.pallas_call(
        functools.partial(_inproj_kernel, has_y),
        grid=(b, t // tm), in_specs=in_specs, out_specs=out_specs, out_shape=out_shape,
        compiler_params=_cparams(("arbitrary", "arbitrary")),
    )(*args)
    return outs if has_y else [x] + list(outs)


def _rope_tables(n):
    t = jnp.arange(n, dtype=jnp.int32)
    row = (t // GRID_W).astype(F32)
    col = (t % GRID_W).astype(F32)
    n_freq = HEAD_DIM // 4
    inv = ROPE_BASE ** (-jnp.arange(n_freq, dtype=F32) / n_freq)
    ar = row[:, None] * inv
    ac = col[:, None] * inv
    ang = jnp.concatenate([ar, ar, ac, ac], axis=-1)
    cos = jnp.tile(jnp.cos(ang), (1, LANES // HEAD_DIM))
    sin = jnp.tile(jnp.sin(ang), (1, LANES // HEAD_DIM))
    quarter = (np.arange(LANES) // 16) % 2
    sa = sin * jnp.asarray(np.where(quarter == 0, -1.0, 0.0), F32)
    sb = sin * jnp.asarray(np.where(quarter == 1, 1.0, 0.0), F32)
    return cos, sa, sb


def _permute_w_in(w):
    qkv = w[..., :QKV_COLS]
    hg = w[..., QKV_COLS:QKV_COLS + 5 * HG_DIM]
    cv = w[..., QKV_COLS + 5 * HG_DIM:]
    return jnp.concatenate([qkv, cv, hg], axis=-1).astype(BF16)


def _na_bias_tiles(rpb, rows):
    nq, nk = NA_QROWS * GRID_W, NA_KROWS * GRID_W
    qi = np.repeat(np.arange(NA_QROWS), GRID_W)[:, None]
    qc = np.tile(np.arange(GRID_W), NA_QROWS)[:, None]
    kj = np.repeat(np.arange(NA_KROWS), GRID_W)[None, :]
    kc = np.tile(np.arange(GRID_W), NA_KROWS)[None, :]
    ws = np.clip(qc - NA_COLS // 2, 0, GRID_W - NA_COLS)
    col_ok = (kc >= ws) & (kc < ws + NA_COLS)
    rel_col = np.clip(kc - qc + NA_COLS - 1, 0, 2 * NA_COLS - 2)
    rel_rows, masks = [], []
    for r0 in (0, NA_QROWS, rows - NA_QROWS):
        kb = int(np.clip(r0 - NA_ROWS // 2, 0, rows - NA_KROWS))
        qr, kr = r0 + qi, kb + kj
        rs = np.clip(qr - NA_ROWS // 2, 0, rows - NA_ROWS)
        masks.append((kr >= rs) & (kr < rs + NA_ROWS) & col_ok)
        rel_rows.append(np.clip(kr - qr + NA_ROWS - 1, 0, 2 * NA_ROWS - 2))
    rel_row = np.stack(rel_rows).astype(np.int32)
    mask = np.stack(masks)
    bias = rpb[:, rel_row, np.broadcast_to(rel_col, (3, nq, nk)).astype(np.int32)]
    bias = jnp.where(jnp.asarray(mask)[None], bias, NEG_INF)
    return bias.transpose(1, 0, 2, 3)


def _softmax_pv(scores, values):
    m = scores[0].max(axis=-1, keepdims=True)
    for s in scores[1:]:
        m = jnp.maximum(m, s.max(axis=-1, keepdims=True))
    acc, den = None, None
    for s, v in zip(scores, values):
        p = jnp.exp(s - m)
        d = p.sum(axis=-1, keepdims=True)
        o = _dot(p.astype(BF16), v)
        acc = o if acc is None else acc + o
        den = d if den is None else den + d
    return acc / den


def _na_kernel(rows, q_ref, k_ref, v_ref, kc_ref, vc_ref, bias_ref, o_ref):
    i = pl.program_id(2)
    kb = jnp.clip(i * NA_QROWS - NA_ROWS // 2, 0, rows - NA_KROWS)
    start = pl.multiple_of(kb * GRID_W, GRID_W)
    q = q_ref[0]
    kt = k_ref[0, pl.ds(start, NA_KROWS * GRID_W), :]
    vt = v_ref[0, pl.ds(start, NA_KROWS * GRID_W), :]
    kc = kc_ref[0]
    vc = vc_ref[0]
    lane = lax.broadcasted_iota(jnp.int32, q.shape, 1)
    outs = []
    for hh in range(LANES // HEAD_DIM):
        qh = jnp.where((lane // HEAD_DIM) == hh, q, jnp.zeros_like(q))
        s_loc = _dot_nt(qh, kt) + bias_ref[0, hh]
        s_ctx = _dot_nt(qh, kc)
        outs.append(_softmax_pv([s_loc, s_ctx], [vt, vc]))
    o_ref[0] = jnp.where(lane < HEAD_DIM, outs[0], outs[1])


def _na_attention(qkv, qkv_c, bias):
    b, t, _ = qkv.shape
    ctx_len = qkv_c.shape[1]
    rows = t // GRID_W
    nq, nk = NA_QROWS * GRID_W, NA_KROWS * GRID_W
    nt = t // nq
    nhp = NA_DIM // LANES

    def variant(i):
        return jnp.where(i == 0, 0, jnp.where(i == nt - 1, 2, 1))

    return pl.pallas_call(
        functools.partial(_na_kernel, rows),
        grid=(b, nhp, nt),
        in_specs=[
            pl.BlockSpec((1, nq, LANES), lambda bi, hp, i: (bi, i, hp)),
            pl.BlockSpec((1, t, LANES), lambda bi, hp, i: (bi, 0, nhp + hp)),
            pl.BlockSpec((1, t, LANES), lambda bi, hp, i: (bi, 0, 2 * nhp + hp)),
            pl.BlockSpec((1, ctx_len, LANES), lambda bi, hp, i: (bi, 0, nhp + hp)),
            pl.BlockSpec((1, ctx_len, LANES), lambda bi, hp, i: (bi, 0, 2 * nhp + hp)),
            pl.BlockSpec((1, LANES // HEAD_DIM, nq, nk), lambda bi, hp, i: (variant(i), hp, 0, 0)),
        ],
        out_specs=pl.BlockSpec((1, nq, LANES), lambda bi, hp, i: (bi, i, hp)),
        out_shape=jax.ShapeDtypeStruct((b, t, NA_DIM), F32),
        compiler_params=_cparams(("arbitrary", "arbitrary", "arbitrary")),
    )(qkv, qkv, qkv, qkv_c, qkv_c, bias)


def _ctx_attn_kernel(q_ref, k_ref, v_ref, o_ref):
    q = q_ref[0]
    lane = lax.broadcasted_iota(jnp.int32, q.shape, 1)
    outs = []
    for hh in range(LANES // HEAD_DIM):
        qh = jnp.where((lane // HEAD_DIM) == hh, q, jnp.zeros_like(q))
        outs.append(_softmax_pv([_dot_nt(qh, k_ref[0])], [v_ref[0]]))
    o_ref[0] = jnp.where(lane < HEAD_DIM, outs[0], outs[1])


def _ctx_attention(qkv_c):
    b, n, _ = qkv_c.shape
    nhp = NA_DIM // LANES
    return pl.pallas_call(
        _ctx_attn_kernel,
        grid=(b, nhp),
        in_specs=[pl.BlockSpec((1, n, LANES), lambda bi, hp: (bi, 0, hp)),
                  pl.BlockSpec((1, n, LANES), lambda bi, hp: (bi, 0, nhp + hp)),
                  pl.BlockSpec((1, n, LANES), lambda bi, hp: (bi, 0, 2 * nhp + hp))],
        out_specs=pl.BlockSpec((1, n, LANES), lambda bi, hp: (bi, 0, hp)),
        out_shape=jax.ShapeDtypeStruct((b, n, NA_DIM), F32),
        compiler_params=_cparams(("arbitrary", "arbitrary")),
    )(qkv_c, qkv_c, qkv_c)


REST_LANE_BLOCK = {"q": 6, "f_fwd": 9, "f_bwd": 12, "i": 15, "g": 18}


def _gla_gates(z, lbl, l1m, omlb):
    e = jnp.exp(-jnp.abs(z))
    log_sig = jnp.minimum(z, 0.0) - jnp.log1p(e)
    y = l1m + log_sig
    hi = jnp.maximum(lbl, y)
    lo = jnp.minimum(lbl, y)
    log_f = hi + jnp.log1p(jnp.exp(lo - hi))
    k = omlb * (jnp.where(z >= 0, e, 1.0) / (1.0 + e))
    return log_f, k


def _gla_chunk(q, k, g, v, st, rev):
    n, sub = HG_CHUNK, HG_SUB
    lane = lax.broadcasted_iota(jnp.int32, (sub, LANES), 1)
    trow = lax.broadcasted_iota(jnp.int32, (sub, LANES), 0)
    ti = lax.broadcasted_iota(jnp.int32, (n, n), 0)
    si = lax.broadcasted_iota(jnp.int32, (n, n), 1)
    tri = ((ti <= si) if rev else (ti >= si)).astype(BF16)
    a = _dot_sel_rhs(tri, g)
    ones = _head_ones()
    o_parts = []
    for i in range(n // sub):
        lo = i * sub
        a_i, q_i = a[lo:lo + sub], q[lo:lo + sub]
        prods = []
        for sl in range(sub):
            s = lo + sl
            keep = (trow <= sl) if rev else (trow >= sl)
            e = jnp.exp(jnp.where(keep, a_i - a[s:s + 1], NEG_INF))
            prods.append(q_i * k[s:s + 1] * e)
        sc = _dot(jnp.concatenate(prods, axis=0).astype(BF16), ones)
        o_i = sc[0:sub] * v[lo:lo + 1]
        for sl in range(1, sub):
            o_i = o_i + sc[sl * sub:(sl + 1) * sub] * v[lo + sl:lo + sl + 1]
        other = slice(lo + sub, n) if rev else slice(0, lo)
        if other.stop > other.start:
            r = a[lo + sub - 1:lo + sub] if rev else a[lo:lo + 1]
            qd = q_i * jnp.exp(a_i - r)
            kd = (k[other] * jnp.exp(r - a[other])).astype(BF16)
            q2 = jnp.concatenate([jnp.where(lane < HEAD_DIM, qd, 0.0), jnp.where(lane >= HEAD_DIM, qd, 0.0)], axis=0)
            o2 = _dot(_dot_nt(q2.astype(BF16), kd).astype(BF16), v[other].astype(BF16))
            o_i = o_i + jnp.where(lane < HEAD_DIM, o2[:sub], o2[sub:])
        o_parts.append(o_i)
    o = jnp.concatenate(o_parts, axis=0)
    o = o + _dot_nt((q * jnp.exp(a)).astype(BF16), st.astype(BF16))
    a_end = a[0:1] if rev else a[n - 1:n]
    kd = (k * jnp.exp(a_end - a)).astype(BF16)
    u = lax.dot_general(v.astype(BF16), kd, (((0,), (0,)), ((), ())), preferred_element_type=F32)
    bi = lax.broadcasted_iota(jnp.int32, (LANES, LANES), 0) // HEAD_DIM
    bj = lax.broadcasted_iota(jnp.int32, (LANES, LANES), 1) // HEAD_DIM
    st_new = st * jnp.exp(a_end) + jnp.where(bi == bj, u, 0.0)
    return o, st_new


def _gla_kernel(rev, final, *refs):
    if final:
        (q_ref, f_ref, v_ref, qc_ref, fc_ref, vc_ref, lbl_ref, l1m_ref, omlb_ref,
         op_ref, g_ref, opc_ref, gc_ref, nw_ref, o_ref, oc_ref, st_ref) = refs
    else:
        (q_ref, f_ref, v_ref, qc_ref, fc_ref, vc_ref, lbl_ref, l1m_ref, omlb_ref,
         o_ref, oc_ref, st_ref) = refs
    step = pl.program_id(2)

    def scan_block(qr, fr, vr, outr, prevr, gater):
        nchunk = qr.shape[1] // HG_CHUNK

        def body(j, carry):
            c = (nchunk - 1 - j) if rev else j
            rows = pl.ds(pl.multiple_of(c * HG_CHUNK, HG_CHUNK), HG_CHUNK)
            log_f, k = _gla_gates(fr[0, rows, :], lbl_ref[...], l1m_ref[...], omlb_ref[...])
            q = qr[0, rows, :] * (HEAD_DIM ** -0.5)
            o, st_new = _gla_chunk(q, k, log_f, vr[0, rows, :], st_ref[...], rev)
            st_ref[...] = st_new
            if final:
                tot = o + prevr[0, rows, :]
                ms = _dot_sel_lhs(tot * tot, _head_ones()) * (1.0 / HEAD_DIM)
                o = tot * lax.rsqrt(ms + NORM_EPS) * nw_ref[...] * _silu(gater[0, rows, :])
            outr[0, rows, :] = o
            return carry

        lax.fori_loop(0, nchunk, body, 0)

    @pl.when(step == 0)
    def _():
        st_ref[...] = jnp.zeros_like(st_ref)
        scan_block(qc_ref, fc_ref, vc_ref, oc_ref, opc_ref if final else None, gc_ref if final else None)

    @pl.when(step > 0)
    def _():
        scan_block(q_ref, f_ref, v_ref, o_ref, op_ref if final else None, g_ref if final else None)


def _gla_scan(rest, rest_c, lbl, l1m, omlb, rev, prev=None, prev_c=None, norm_w=None, tb=512):
    b, t, _ = rest.shape
    ctx_len = rest_c.shape[1]
    nblk = t // tb
    nhp = HG_DIM // LANES
    final = prev is not None
    fkey = "f_bwd" if rev else "f_fwd"

    def blk(s):
        return (nblk - jnp.maximum(s, 1)) if rev else jnp.maximum(s - 1, 0)

    def lat(key):
        off = REST_LANE_BLOCK[key]
        return pl.BlockSpec((1, tb, LANES), lambda bi, hp, s: (bi, blk(s), off + hp))

    def ctx(key):
        off = REST_LANE_BLOCK[key]
        return pl.BlockSpec((1, ctx_len, LANES), lambda bi, hp, s: (bi, 0, off + hp))

    vec = pl.BlockSpec((1, LANES), lambda bi, hp, s: (0, hp))
    o_lat = pl.BlockSpec((1, tb, LANES), lambda bi, hp, s: (bi, blk(s), hp))
    o_ctx = pl.BlockSpec((1, ctx_len, LANES), lambda bi, hp, s: (bi, 0, hp))
    in_specs = [lat("q"), lat(fkey), lat("i"), ctx("q"), ctx(fkey), ctx("i"), vec, vec, vec]
    args = [rest, rest, rest, rest_c, rest_c, rest_c, lbl, l1m, omlb]
    if final:
        in_specs += [o_lat, lat("g"), o_ctx, ctx("g"), pl.BlockSpec((1, LANES), lambda bi, hp, s: (0, 0))]
        args += [prev, rest, prev_c, rest_c, norm_w]
    return pl.pallas_call(
        functools.partial(_gla_kernel, rev, final),
        grid=(b, nhp, nblk + 1),
        in_specs=in_specs,
        out_specs=[o_lat, o_ctx],
        out_shape=[jax.ShapeDtypeStruct((b, t, HG_DIM), F32), jax.ShapeDtypeStruct((b, ctx_len, HG_DIM), F32)],
        scratch_shapes=[pltpu.VMEM((LANES, LANES), F32)],
        compiler_params=_cparams(("arbitrary", "arbitrary", "arbitrary")),
    )(*args)


HALO = 8


def _outproj_kernel(na_ref, hg_ref, cb_ref, cc_ref, cu_ref, ccp_ref, cup_ref, ccn_ref, cun_ref, x_ref,
                    w_ref, g_ref, sh_ref, sc_ref, nw_ref, cw_ref, wr_ref, xo_ref, h_ref, aff_ref):
    i = pl.program_id(1)
    last = pl.num_programs(1) - 1
    z = cc_ref[0] * cu_ref[0]
    tm = z.shape[0]
    row = lax.broadcasted_iota(jnp.int32, z.shape, 0)
    z_before = jnp.where(i > 0, ccp_ref[0, HALO - 1:HALO, :] * cup_ref[0, HALO - 1:HALO, :], 0.0)
    z_after = jnp.where(i < last, ccn_ref[0, 0:1, :] * cun_ref[0, 0:1, :], 0.0)
    zp = jnp.where(row == 0, z_before, pltpu.roll(z, 1, 0))
    zn = jnp.where(row == tm - 1, z_after, pltpu.roll(z, tm - 1, 0))
    cv = cb_ref[0] * (cw_ref[0:1, :] * zp + cw_ref[1:2, :] * z + cw_ref[2:3, :] * zn)
    mix = jnp.concatenate([na_ref[0], hg_ref[0], cv], axis=-1).astype(BF16)
    x = x_ref[0] + g_ref[0] * _dot(mix, w_ref[...])
    xo_ref[0] = x
    h = _rms_rows(x, nw_ref[...]) * (1.0 + sc_ref[0]) + sh_ref[0]
    h_ref[0] = h
    logits = _dot_nt(wr_ref[...], h.astype(BF16))
    p = jnp.exp(logits - logits.max(axis=0, keepdims=True))
    aff_ref[0] = p / p.sum(axis=0, keepdims=True)


def _outproj(na, hg, rest, x, w_out, gate, shift, scale, norm_w, conv_w, w_router_t, tm):
    b, t, d = x.shape
    nh = tm // HALO
    row = pl.BlockSpec((1, tm, d), lambda bi, i: (bi, i, 0))
    vec = pl.BlockSpec((1, 1, d), lambda bi, i: (bi, 0, 0))
    mixer = pl.BlockSpec((1, tm, NA_DIM), lambda bi, i: (bi, i, 0))

    def conv(c):
        return pl.BlockSpec((1, tm, CONV_DIM), lambda bi, i: (bi, i, c))

    def before(c):
        return pl.BlockSpec((1, HALO, CONV_DIM), lambda bi, i: (bi, jnp.maximum(i * nh - 1, 0), c))

    def after(c):
        return pl.BlockSpec((1, HALO, CONV_DIM), lambda bi, i: (bi, jnp.minimum((i + 1) * nh, t // HALO - 1), c))

    return pl.pallas_call(
        _outproj_kernel,
        grid=(b, t // tm),
        in_specs=[mixer, mixer, conv(0), conv(1), conv(2), before(1), before(2), after(1), after(2), row,
                  pl.BlockSpec(w_out.shape, lambda bi, i: (0, 0)), vec, vec, vec,
                  pl.BlockSpec((1, d), lambda bi, i: (0, 0)),
                  pl.BlockSpec(conv_w.shape, lambda bi, i: (0, 0)),
                  pl.BlockSpec(w_router_t.shape, lambda bi, i: (0, 0))],
        out_specs=[row, row, pl.BlockSpec((1, N_EXPERTS, tm), lambda bi, i: (bi, 0, i))],
        out_shape=[jax.ShapeDtypeStruct((b, t, d), F32), jax.ShapeDtypeStruct((b, t, d), F32),
                   jax.ShapeDtypeStruct((b, N_EXPERTS, t), F32)],
        compiler_params=_cparams(("arbitrary", "arbitrary")),
    )(na, hg, rest, rest, rest, rest, rest, rest, rest, x, w_out, gate, shift, scale, norm_w, conv_w, w_router_t)


def _route_kernel(cap, aff_ref, idx_ref, val_ref, off_ref, w_ref, tot_ref, roff_ref):
    ne, nr, _ = aff_ref.shape[1:]
    capp = idx_ref.shape[2]
    a = aff_ref[0]
    bits = pltpu.bitcast(a, jnp.int32)

    def count(m):
        return jnp.sum(jnp.sum(m.astype(F32), axis=2, keepdims=True), axis=1, keepdims=True)

    def search(k, thr):
        cand = thr | jnp.left_shift(jnp.int32(1), 30 - k)
        return jnp.where(count(bits >= cand) >= cap, cand, thr)

    thr = lax.fori_loop(0, 31, search, jnp.zeros((ne, 1, 1), jnp.int32))
    gt = bits > thr
    eq = bits == thr
    need = cap - count(gt)

    li = lax.broadcasted_iota(jnp.int32, (LANES, LANES), 0)
    lj = lax.broadcasted_iota(jnp.int32, (LANES, LANES), 1)
    before_lane = (li < lj).astype(BF16)
    upto_lane = (li <= lj).astype(BF16)
    all_lanes = jnp.ones((LANES, LANES), BF16)
    ri = lax.broadcasted_iota(jnp.int32, (nr, nr), 0)
    rj = lax.broadcasted_iota(jnp.int32, (nr, nr), 1)
    before_row = (rj < ri).astype(BF16)

    def prefix(m, lane_mat):
        m2 = m.astype(BF16).reshape(ne * nr, LANES)
        within = _dot(m2, lane_mat).reshape(ne, nr, LANES)
        tot = _dot(m2, all_lanes).reshape(ne, nr, LANES)
        roff = jnp.stack([_dot(before_row, tot[e].astype(BF16)) for e in range(ne)])
        return within, tot, roff

    w_eq, _, r_eq = prefix(eq, before_lane)
    sel = gt | (eq & (w_eq + r_eq < need))
    w_sel, t_sel, r_sel = prefix(sel, upto_lane)
    off_ref[0] = r_sel.astype(jnp.int32)
    w_ref[...] = w_sel
    tot_ref[...] = t_sel
    roff_ref[...] = r_sel

    slot = lax.broadcasted_iota(jnp.int32, (nr, capp), 1).astype(F32)
    rowid = lax.broadcasted_iota(jnp.int32, (nr, capp), 0).astype(F32)
    laneid = lax.broadcasted_iota(jnp.int32, (LANES, capp), 0).astype(F32)
    tn = (((0,), (0,)), ((), ()))

    def per_expert(e, carry):
        roff = roff_ref[e][:, 0:1]
        rincl = roff + tot_ref[e][:, 0:1]
        row_of = jnp.sum((rincl <= slot).astype(F32), axis=0, keepdims=True)
        onehot = rowid == row_of
        base = jnp.sum(jnp.where(onehot, roff, 0.0), axis=0, keepdims=True)
        onehot = onehot.astype(BF16)
        counts = lax.dot_general(w_ref[e].astype(BF16), onehot, tn, preferred_element_type=F32)
        col_of = jnp.sum((counts <= slot[0:1] - base).astype(F32), axis=0, keepdims=True)
        hi, mid, lo = _split3(aff_ref[0, e])
        vals = (lax.dot_general(hi, onehot, tn, preferred_element_type=F32)
                + lax.dot_general(mid, onehot, tn, preferred_element_type=F32)
                + lax.dot_general(lo, onehot, tn, preferred_element_type=F32))
        val = jnp.sum(jnp.where(laneid == col_of, vals, 0.0), axis=0, keepdims=True)
        live = slot[0:1] < cap
        idx_ref[0, pl.ds(e, 1), :] = jnp.where(live, row_of * LANES + col_of, 0.0).astype(jnp.int32)
        val_ref[0, pl.ds(e, 1), :] = jnp.where(live, val, 0.0)
        return carry

    lax.fori_loop(0, ne, per_expert, 0)


def _route(aff4, cap):
    b, ne, nr, _ = aff4.shape
    capp = -(-cap // LANES) * LANES
    blk = pl.BlockSpec((1, ne, nr, LANES), lambda bi: (bi, 0, 0, 0))
    lst = pl.BlockSpec((1, ne, capp), lambda bi: (bi, 0, 0))
    return pl.pallas_call(
        functools.partial(_route_kernel, cap),
        grid=(b,),
        in_specs=[blk],
        out_specs=[lst, lst, blk],
        out_shape=[jax.ShapeDtypeStruct((b, ne, capp), jnp.int32), jax.ShapeDtypeStruct((b, ne, capp), F32),
                   jax.ShapeDtypeStruct((b, ne, nr, LANES), jnp.int32)],
        scratch_shapes=[pltpu.VMEM((ne, nr, LANES), F32)] * 3,
        compiler_params=_cparams(("arbitrary",)),
    )(aff4)


MOE_SLOTS = 256
MOE_RANGE = 2048


def _moe_kernel(nrange, lo_ref, idx_ref, val_ref, h_ref, wg_ref, wu_ref, wd_ref, o_ref, xg_ref, yb_ref):
    bi, r, e = pl.program_id(0), pl.program_id(1), pl.program_id(2)
    rng = h_ref.shape[1]

    @pl.when(e == 0)
    def _():
        o_ref[...] = jnp.zeros_like(o_ref)

    base = (bi * pl.num_programs(2) + e) * (nrange + 1) + r
    lo, hi = lo_ref[base], lo_ref[base + 1]
    tok0 = r * rng

    def chunk(ci, carry):
        j0 = lo + ci * MOE_SLOTS
        cnt = jnp.minimum(MOE_SLOTS, hi - j0)
        xg_ref[...] = jnp.zeros_like(xg_ref)

        def gather(jj, c):
            tok = idx_ref[0, 0, j0 + jj] - tok0
            xg_ref[pl.ds(jj, 1), :] = h_ref[0, pl.ds(tok, 1), :]
            return c

        lax.fori_loop(0, cnt, gather, 0)
        xb = xg_ref[...].astype(BF16)
        hid = _silu(_dot(xb, wg_ref[0])) * _dot(xb, wu_ref[0])
        yb_ref[...] = _dot(hid.astype(BF16), wd_ref[0])

        def scatter(jj, c):
            tok = idx_ref[0, 0, j0 + jj] - tok0
            o_ref[0, pl.ds(tok, 1), :] += val_ref[0, 0, j0 + jj] * yb_ref[pl.ds(jj, 1), :]
            return c

        lax.fori_loop(0, cnt, scatter, 0)
        return carry

    lax.fori_loop(0, (hi - lo + MOE_SLOTS - 1) // MOE_SLOTS, chunk, 0)


def _moe(h2, idx, vals, off, cap, wg, wu, wd, rng):
    b, t, d = h2.shape
    ne, _, f = wg.shape
    nrange = t // rng
    capp = idx.shape[-1]
    lo = off[:, :, ::rng // LANES, 0][:, :, :nrange]
    lo = jnp.concatenate([lo, jnp.full((b, ne, 1), cap, jnp.int32)], axis=-1).reshape(-1)
    lst = pl.BlockSpec((1, 1, capp), lambda bi, r, e, lo_ref: (bi * ne + e, 0, 0), memory_space=pltpu.SMEM)
    tile = pl.BlockSpec((1, rng, d), lambda bi, r, e, lo_ref: (bi, r, 0))
    return pl.pallas_call(
        functools.partial(_moe_kernel, nrange),
        grid_spec=pltpu.PrefetchScalarGridSpec(
            num_scalar_prefetch=1,
            grid=(b, nrange, ne),
            in_specs=[lst, lst, tile,
                      pl.BlockSpec((1, d, f), lambda bi, r, e, lo_ref: (e, 0, 0)),
                      pl.BlockSpec((1, d, f), lambda bi, r, e, lo_ref: (e, 0, 0)),
                      pl.BlockSpec((1, f, d), lambda bi, r, e, lo_ref: (e, 0, 0))],
            out_specs=tile,
            scratch_shapes=[pltpu.VMEM((MOE_SLOTS, d), F32), pltpu.VMEM((MOE_SLOTS, d), F32)]),
        out_shape=jax.ShapeDtypeStruct((b, t, d), F32),
        compiler_params=_cparams(("arbitrary", "arbitrary", "arbitrary")),
    )(lo, idx.reshape(b * ne, 1, capp), vals.reshape(b * ne, 1, capp), h2, wg, wu, wd)


def _residual_kernel(x_ref, y_ref, g_ref, o_ref):
    o_ref[0] = x_ref[0] + g_ref[0] * y_ref[0]


def _residual(x, y, gate, tm):
    b, t, d = x.shape
    row = pl.BlockSpec((1, tm, d), lambda bi, i: (bi, i, 0))
    return pl.pallas_call(
        _residual_kernel, grid=(b, t // tm),
        in_specs=[row, row, pl.BlockSpec((1, 1, d), lambda bi, i: (bi, 0, 0))],
        out_specs=row, out_shape=jax.ShapeDtypeStruct((b, t, d), F32),
        compiler_params=_cparams(("arbitrary", "arbitrary")),
    )(x, y, gate)


def _hgrn_lower_bounds(logits):
    p = jax.nn.softmax(logits.astype(F32), axis=1)
    return jnp.concatenate([jnp.zeros_like(p[:, :1]), jnp.cumsum(p[:, 1:], axis=1)], axis=1)


def _expert_choice_ffn(h2, aff, wg, wu, wd, rng):
    b, ne, t = aff.shape
    cap = EC_CAPACITY_FACTOR * t // ne
    rows = max(t // LANES, 8)
    pad = rows * LANES - t
    if pad:
        aff = jnp.pad(aff, ((0, 0), (0, 0), (0, pad)), constant_values=-1.0)
    idx, vals, off = _route(aff.reshape(b, ne, rows, LANES), cap)
    return _moe(h2, idx, vals, off, cap, wg, wu, wd, rng)


def kernel(x, c, ctx, c_ctx, w_mod, b_mod, norm1_w, w_in, na_q_norm, na_k_norm, na_rpb, hg_lb_logits, hg_norm,
           conv_w, w_out, norm2_w, w_router, w_exp_gate, w_exp_up, w_exp_down):
    b, t, d = x.shape
    wg, wu, wd = w_exp_gate.astype(BF16), w_exp_up.astype(BF16), w_exp_down.astype(BF16)
    depth = w_mod.shape[0]
    ctx_len = ctx.shape[1]
    cond = jnp.zeros((8, d), F32).at[0:b].set(c).at[b].set(c_ctx)
    mod = _modvec(cond, w_mod, b_mod).reshape(depth, 8, N_MOD, d)
    cos, sa, sb = _rope_tables(t)
    cos_c = jnp.ones((ctx_len, LANES), F32)
    zero_c = jnp.zeros((ctx_len, LANES), F32)
    lb = _hgrn_lower_bounds(hg_lb_logits)
    w_in_p = _permute_w_in(w_in)
    w_out_b = w_out.astype(BF16)
    w_router_t = w_router.transpose(0, 2, 1).astype(BF16)
    bias = jax.vmap(lambda r: _na_bias_tiles(r, t // GRID_W))(na_rpb)
    tile2 = lambda v: jnp.tile(v, LANES // HEAD_DIM)[None]

    cx = ctx
    y = y_c = None
    for l in range(depth):
        need_ctx = l < depth - 1
        m_lat = [mod[l, 0:b, j][:, None, :] for j in range(N_MOD)]
        m_ctx = [jnp.broadcast_to(mod[l, b, j][None, None, :], (b, 1, d)) for j in range(N_MOD)]
        g_prev = [mod[l - 1, 0:b, 5][:, None, :], jnp.broadcast_to(mod[l - 1, b, 5][None, None, :], (b, 1, d))] if l else [None, None]
        qn, kn = tile2(na_q_norm[l]), tile2(na_k_norm[l])
        x, qkv, rest = _inproj(x, y, g_prev[0], m_lat[0], m_lat[1], norm1_w[l][None], w_in_p[l], cos, sa, sb, qn, kn, 512)
        cx, qkv_c, rest_c = _inproj(cx, y_c, g_prev[1], m_ctx[0], m_ctx[1], norm1_w[l][None], w_in_p[l],
                                    cos_c, zero_c, zero_c, qn, kn, ctx_len)
        na = _na_attention(qkv, qkv_c, bias[l])
        o = o_c = None
        for rev in (False, True):
            lbd = lb[1 if rev else 0, l][None]
            o, o_c = _gla_scan(rest, rest_c, jnp.log(lbd), jnp.log1p(-lbd), 1.0 - lbd, rev, o, o_c,
                               tile2(hg_norm[l]) if rev else None)
        x, h2, aff = _outproj(na, o, rest, x, w_out_b[l], m_lat[2], m_lat[3], m_lat[4], norm2_w[l][None],
                              conv_w[l], w_router_t[l], 512)
        y = _expert_choice_ffn(h2, aff, wg[l], wu[l], wd[l], min(MOE_RANGE, t))
        if need_ctx:
            na_c = _ctx_attention(qkv_c)
            cx, h2_c, aff_c = _outproj(na_c, o_c, rest_c, cx, w_out_b[l], m_ctx[2], m_ctx[3], m_ctx[4],
                                       norm2_w[l][None], conv_w[l], w_router_t[l], ctx_len)
            y_c = _expert_choice_ffn(h2_c, aff_c, wg[l], wu[l], wd[l], ctx_len)
    return _residual(x, y, mod[depth - 1, 0:b, 5][:, None, :], 512)
```

```python
import functools

import jax
import jax.numpy as jnp
import numpy as np
from jax import lax
from jax.experimental import pallas as pl
from jax.experimental.pallas import tpu as pltpu

GRID_W = 64
HEAD_DIM = 64
NA_HEADS = 6
NA_DIM = NA_HEADS * HEAD_DIM
NA_ROWS = 8
NA_COLS = 16
HG_HEADS = 6
HG_DIM = HG_HEADS * HEAD_DIM
HG_CHUNK = 64
HG_SUB = 16
CONV_DIM = 256
N_EXPERTS = 16
EC_CAPACITY_FACTOR = 2
ROPE_BASE = 10000.0
NORM_EPS = 1e-6
N_MOD = 6

LANES = 128
NA_QROWS = 4
NA_KROWS = NA_QROWS + NA_ROWS - 1
VMEM_LIMIT = 56 * 1024 * 1024

F32 = jnp.float32
BF16 = jnp.bfloat16
NEG_INF = float("-inf")


def _cparams(sem):
    return pltpu.CompilerParams(dimension_semantics=sem, vmem_limit_bytes=VMEM_LIMIT)


def _dot(a, b):
    return jnp.dot(a, b, preferred_element_type=F32)


def _dot_nt(a, b):
    return lax.dot_general(a, b, (((1,), (1,)), ((), ())), preferred_element_type=F32)


def _split3(x):
    hi = x.astype(BF16)
    r1 = x - hi.astype(F32)
    mid = r1.astype(BF16)
    lo = (r1 - mid.astype(F32)).astype(BF16)
    return hi, mid, lo


def _dot_sel_rhs(sel, x):
    hi, mid, lo = _split3(x)
    return _dot(sel, hi) + _dot(sel, mid) + _dot(sel, lo)


def _dot_sel_lhs(x, sel):
    hi, mid, lo = _split3(x)
    return _dot(hi, sel) + _dot(mid, sel) + _dot(lo, sel)


def _head_ones():
    r = lax.broadcasted_iota(jnp.int32, (LANES, LANES), 0) // HEAD_DIM
    c = lax.broadcasted_iota(jnp.int32, (LANES, LANES), 1) // HEAD_DIM
    return (r == c).astype(BF16)


def _silu(x):
    return x * (1.0 / (1.0 + jnp.exp(-x)))


def _modvec_kernel(cond_ref, w_ref, b_ref, o_ref):
    a = _silu(cond_ref[...]).astype(BF16)
    o_ref[0] = _dot(a, w_ref[0].astype(BF16)) + b_ref[0]


def _modvec(cond, w_mod, b_mod):
    depth, d, n = w_mod.shape
    tn = 512
    return pl.pallas_call(
        _modvec_kernel, name="modvec",
        grid=(depth, n // tn),
        in_specs=[
            pl.BlockSpec((8, d), lambda l, j: (0, 0)),
            pl.BlockSpec((1, d, tn), lambda l, j: (l, 0, j)),
            pl.BlockSpec((1, 1, tn), lambda l, j: (l, 0, j)),
        ],
        out_specs=pl.BlockSpec((1, 8, tn), lambda l, j: (l, 0, j)),
        out_shape=jax.ShapeDtypeStruct((depth, 8, n), F32),
        compiler_params=_cparams(("arbitrary", "arbitrary")),
    )(cond, w_mod, b_mod.reshape(depth, 1, n))


QKV_COLS = 3 * NA_DIM
REST_COLS = 3 * CONV_DIM + 5 * HG_DIM


def _rms_rows(x, w):
    ms = jnp.mean(x * x, axis=-1, keepdims=True)
    return x * lax.rsqrt(ms + NORM_EPS) * w


def _inproj_kernel(has_y, *refs):
    if has_y:
        (x_ref, y_ref, g_ref, sh_ref, sc_ref, nw_ref, w_ref, cos_ref, sa_ref, sb_ref, qn_ref, kn_ref,
         xo_ref, qkv_ref, rest_ref) = refs
        x = x_ref[0] + g_ref[0] * y_ref[0]
        xo_ref[0] = x
    else:
        (x_ref, sh_ref, sc_ref, nw_ref, w_ref, cos_ref, sa_ref, sb_ref, qn_ref, kn_ref,
         qkv_ref, rest_ref) = refs
        x = x_ref[0]
    h = _rms_rows(x, nw_ref[...]) * (1.0 + sc_ref[0]) + sh_ref[0]
    hb = h.astype(BF16)
    ones = _head_ones()
    cos = cos_ref[...]
    sa = sa_ref[...]
    sb = sb_ref[...]
    for c in range(2 * NA_DIM // LANES):
        p = _dot(hb, w_ref[:, c * LANES:(c + 1) * LANES])
        ssq = _dot_sel_lhs(p * p, ones)
        nw = qn_ref[...] if c < NA_DIM // LANES else kn_ref[...]
        pn = p * lax.rsqrt(ssq * (1.0 / HEAD_DIM) + NORM_EPS) * nw
        pr = pn * cos + pltpu.roll(pn, LANES - 16, 1) * sa + pltpu.roll(pn, 16, 1) * sb
        if c < NA_DIM // LANES:
            pr = pr * (HEAD_DIM ** -0.5)
        qkv_ref[0, :, c * LANES:(c + 1) * LANES] = pr.astype(BF16)
    qkv_ref[0, :, 2 * NA_DIM:] = _dot(hb, w_ref[:, 2 * NA_DIM:QKV_COLS]).astype(BF16)
    for c in range(REST_COLS // HG_DIM):
        lo = QKV_COLS + c * HG_DIM
        rest_ref[0, :, c * HG_DIM:(c + 1) * HG_DIM] = _dot(hb, w_ref[:, lo:lo + HG_DIM])


def _inproj(x, y, gate, shift, scale, norm_w, w_in, cos, sa, sb, qn, kn, tm):
    b, t, d = x.shape
    has_y = y is not None
    row = pl.BlockSpec((1, tm, d), lambda bi, i: (bi, i, 0))
    vec = pl.BlockSpec((1, 1, d), lambda bi, i: (bi, 0, 0))
    tab = pl.BlockSpec((tm, LANES), lambda bi, i: (i, 0))
    small = pl.BlockSpec((1, LANES), lambda bi, i: (0, 0))
    in_specs = [row] + ([row, vec] if has_y else []) + [
        vec, vec, pl.BlockSpec((1, d), lambda bi, i: (0, 0)),
        pl.BlockSpec(w_in.shape, lambda bi, i: (0, 0)), tab, tab, tab, small, small]
    out_specs = ([row] if has_y else []) + [
        pl.BlockSpec((1, tm, QKV_COLS), lambda bi, i: (bi, i, 0)),
        pl.BlockSpec((1, tm, REST_COLS), lambda bi, i: (bi, i, 0))]
    out_shape = ([jax.ShapeDtypeStruct((b, t, d), F32)] if has_y else []) + [
        jax.ShapeDtypeStruct((b, t, QKV_COLS), BF16), jax.ShapeDtypeStruct((b, t, REST_COLS), F32)]
    args = [x] + ([y, gate] if has_y else []) + [shift, scale, norm_w, w_in, cos, sa, sb, qn, kn]
    outs = pl.pallas_call(
        functools.partial(_inproj_kernel, has_y), name="inproj",
        grid=(b, t // tm), in_specs=in_specs, out_specs=out_specs, out_shape=out_shape,
        compiler_params=_cparams(("arbitrary", "arbitrary")),
    )(*args)
    return outs if has_y else [x] + list(outs)


def _rope_tables(n):
    t = jnp.arange(n, dtype=jnp.int32)
    row = (t // GRID_W).astype(F32)
    col = (t % GRID_W).astype(F32)
    n_freq = HEAD_DIM // 4
    inv = ROPE_BASE ** (-jnp.arange(n_freq, dtype=F32) / n_freq)
    ar = row[:, None] * inv
    ac = col[:, None] * inv
    ang = jnp.concatenate([ar, ar, ac, ac], axis=-1)
    cos = jnp.tile(jnp.cos(ang), (1, LANES // HEAD_DIM))
    sin = jnp.tile(jnp.sin(ang), (1, LANES // HEAD_DIM))
    quarter = (np.arange(LANES) // 16) % 2
    sa = sin * jnp.asarray(np.where(quarter == 0, -1.0, 0.0), F32)
    sb = sin * jnp.asarray(np.where(quarter == 1, 1.0, 0.0), F32)
    return cos, sa, sb


def _permute_w_in(w):
    qkv = w[..., :QKV_COLS]
    hg = w[..., QKV_COLS:QKV_COLS + 5 * HG_DIM]
    cv = w[..., QKV_COLS + 5 * HG_DIM:]
    return jnp.concatenate([qkv, cv, hg], axis=-1).astype(BF16)


def _na_bias_tiles(rpb, rows):
    nh, n_rel = rpb.shape[0], 2 * NA_ROWS - 1
    qc = np.arange(GRID_W)[:, None]
    kc = np.arange(GRID_W)[None, :]
    ws = np.clip(qc - NA_COLS // 2, 0, GRID_W - NA_COLS)
    col_ok = (kc >= ws) & (kc < ws + NA_COLS)
    rel_col = np.clip(kc - qc + NA_COLS - 1, 0, 2 * NA_COLS - 2)
    blocks = jnp.where(jnp.asarray(col_ok), rpb[:, :, rel_col], NEG_INF)
    blocks = jnp.concatenate([blocks, jnp.full((nh, 1, GRID_W, GRID_W), NEG_INF, rpb.dtype)], axis=1)
    qi = np.arange(NA_QROWS)[:, None]
    kj = np.arange(NA_KROWS)[None, :]
    which = []
    for r0 in (0, NA_QROWS, rows - NA_QROWS):
        kb = int(np.clip(r0 - NA_ROWS // 2, 0, rows - NA_KROWS))
        qr, kr = r0 + qi, kb + kj
        rs = np.clip(qr - NA_ROWS // 2, 0, rows - NA_ROWS)
        ok = (kr >= rs) & (kr < rs + NA_ROWS)
        which.append(np.where(ok, kr - qr + NA_ROWS - 1, n_rel))
    which = np.stack(which).astype(np.int32)
    tiles = blocks[:, which]
    return tiles.transpose(1, 0, 2, 4, 3, 5).reshape(3, nh, NA_QROWS * GRID_W, NA_KROWS * GRID_W)


def _softmax_pv(scores, values):
    m = scores[0].max(axis=-1, keepdims=True)
    for s in scores[1:]:
        m = jnp.maximum(m, s.max(axis=-1, keepdims=True))
    acc, den = None, None
    for s, v in zip(scores, values):
        p = jnp.exp(s - m)
        d = p.sum(axis=-1, keepdims=True)
        o = _dot(p.astype(BF16), v)
        acc = o if acc is None else acc + o
        den = d if den is None else den + d
    return acc / den


def _na_kernel(rows, q_ref, k_ref, v_ref, kc_ref, vc_ref, bias_ref, o_ref):
    i = pl.program_id(2)
    kb = jnp.clip(i * NA_QROWS - NA_ROWS // 2, 0, rows - NA_KROWS)
    start = pl.multiple_of(kb * GRID_W, GRID_W)
    q = q_ref[0]
    kt = k_ref[0, pl.ds(start, NA_KROWS * GRID_W), :]
    vt = v_ref[0, pl.ds(start, NA_KROWS * GRID_W), :]
    kc = kc_ref[0]
    vc = vc_ref[0]
    lane = lax.broadcasted_iota(jnp.int32, q.shape, 1)
    outs = []
    for hh in range(LANES // HEAD_DIM):
        qh = jnp.where((lane // HEAD_DIM) == hh, q, jnp.zeros_like(q))
        s_loc = _dot_nt(qh, kt) + bias_ref[0, hh]
        s_ctx = _dot_nt(qh, kc)
        outs.append(_softmax_pv([s_loc, s_ctx], [vt, vc]))
    o_ref[0] = jnp.where(lane < HEAD_DIM, outs[0], outs[1])


def _na_attention(qkv, qkv_c, bias):
    b, t, _ = qkv.shape
    ctx_len = qkv_c.shape[1]
    rows = t // GRID_W
    nq, nk = NA_QROWS * GRID_W, NA_KROWS * GRID_W
    nt = t // nq
    nhp = NA_DIM // LANES

    def variant(i):
        return jnp.where(i == 0, 0, jnp.where(i == nt - 1, 2, 1))

    return pl.pallas_call(
        functools.partial(_na_kernel, rows), name="na_attn",
        grid=(b, nhp, nt),
        in_specs=[
            pl.BlockSpec((1, nq, LANES), lambda bi, hp, i: (bi, i, hp)),
            pl.BlockSpec((1, t, LANES), lambda bi, hp, i: (bi, 0, nhp + hp)),
            pl.BlockSpec((1, t, LANES), lambda bi, hp, i: (bi, 0, 2 * nhp + hp)),
            pl.BlockSpec((1, ctx_len, LANES), lambda bi, hp, i: (bi, 0, nhp + hp)),
            pl.BlockSpec((1, ctx_len, LANES), lambda bi, hp, i: (bi, 0, 2 * nhp + hp)),
            pl.BlockSpec((1, LANES // HEAD_DIM, nq, nk), lambda bi, hp, i: (variant(i), hp, 0, 0)),
        ],
        out_specs=pl.BlockSpec((1, nq, LANES), lambda bi, hp, i: (bi, i, hp)),
        out_shape=jax.ShapeDtypeStruct((b, t, NA_DIM), F32),
        compiler_params=_cparams(("arbitrary", "arbitrary", "arbitrary")),
    )(qkv, qkv, qkv, qkv_c, qkv_c, bias)


def _ctx_attn_kernel(q_ref, k_ref, v_ref, o_ref):
    q = q_ref[0]
    lane = lax.broadcasted_iota(jnp.int32, q.shape, 1)
    outs = []
    for hh in range(LANES // HEAD_DIM):
        qh = jnp.where((lane // HEAD_DIM) == hh, q, jnp.zeros_like(q))
        outs.append(_softmax_pv([_dot_nt(qh, k_ref[0])], [v_ref[0]]))
    o_ref[0] = jnp.where(lane < HEAD_DIM, outs[0], outs[1])


def _ctx_attention(qkv_c):
    b, n, _ = qkv_c.shape
    nhp = NA_DIM // LANES
    return pl.pallas_call(
        _ctx_attn_kernel, name="ctx_attn",
        grid=(b, nhp),
        in_specs=[pl.BlockSpec((1, n, LANES), lambda bi, hp: (bi, 0, hp)),
                  pl.BlockSpec((1, n, LANES), lambda bi, hp: (bi, 0, nhp + hp)),
                  pl.BlockSpec((1, n, LANES), lambda bi, hp: (bi, 0, 2 * nhp + hp))],
        out_specs=pl.BlockSpec((1, n, LANES), lambda bi, hp: (bi, 0, hp)),
        out_shape=jax.ShapeDtypeStruct((b, n, NA_DIM), F32),
        compiler_params=_cparams(("arbitrary", "arbitrary")),
    )(qkv_c, qkv_c, qkv_c)


REST_HG_BLOCK = {"q": 2, "f_fwd": 3, "f_bwd": 4, "i": 5, "g": 6}


LOG2E = 1.4426950408889634


def _gla_gates(z, lbl, l1m):
    soft = jnp.log1p(jnp.exp(-jnp.abs(z)))
    y = l1m + (jnp.minimum(z, 0.0) - soft)
    hi = jnp.maximum(lbl, y)
    lo = jnp.minimum(lbl, y)
    log_f = hi + jnp.log1p(jnp.exp(lo - hi))
    log_k = l1m + (jnp.minimum(-z, 0.0) - soft)
    return log_f * LOG2E, log_k * LOG2E


def _gla_chunk(q, lk, g, v, st, rev):
    n, sub, tile = HG_CHUNK, HG_SUB, 8
    width = q.shape[1]
    nheads = width // HEAD_DIM
    pairs = [slice(p * LANES, (p + 1) * LANES) for p in range(width // LANES)]
    head_of_lane = lax.broadcasted_iota(jnp.int32, (sub, width), 1) // HEAD_DIM
    trow = lax.broadcasted_iota(jnp.int32, (tile, width), 0)
    ti = lax.broadcasted_iota(jnp.int32, (n, n), 0)
    si = lax.broadcasted_iota(jnp.int32, (n, n), 1)
    tri = ((ti <= si) if rev else (ti >= si)).astype(BF16)
    a = _dot_sel_rhs(tri, g)
    b = a - lk
    vb = v.astype(BF16)

    qe = (q * jnp.exp2(a)).astype(BF16)
    o_state = jnp.concatenate([_dot_nt(qe[:, p], st[j].astype(BF16)) for j, p in enumerate(pairs)], axis=1)
    a_end = a[0:1] if rev else a[n - 1:n]
    kd_end = jnp.exp2(a_end - b).astype(BF16)
    decay = jnp.exp2(a_end)
    bi = lax.broadcasted_iota(jnp.int32, (LANES, LANES), 0) // HEAD_DIM
    bj = lax.broadcasted_iota(jnp.int32, (LANES, LANES), 1) // HEAD_DIM
    tn = (((0,), (0,)), ((), ()))
    st_new = [st[j] * decay[:, p] + jnp.where(bi == bj, lax.dot_general(vb[:, p], kd_end[:, p], tn,
                                                                        preferred_element_type=F32), 0.0)
              for j, p in enumerate(pairs)]

    o_far = []
    for i in range(n // sub):
        lo = i * sub
        other = slice(lo + sub, n) if rev else slice(0, lo)
        if other.stop == other.start:
            o_far.append(None)
            continue
        r = a[lo + sub - 1:lo + sub] if rev else a[lo:lo + 1]
        qd = q[lo:lo + sub] * jnp.exp2(a[lo:lo + sub] - r)
        kd = jnp.exp2(r - b[other]).astype(BF16)
        stacked = jnp.concatenate([jnp.where(head_of_lane == h, qd, 0.0) for h in range(nheads)], axis=0)
        o_all = _dot(_dot_nt(stacked.astype(BF16), kd).astype(BF16), vb[other])
        pick = o_all[0:sub]
        for h in range(1, nheads):
            pick = jnp.where(head_of_lane == h, o_all[h * sub:(h + 1) * sub], pick)
        o_far.append(pick)

    ones = _head_ones()
    o_parts = []
    for i in range(n // sub):
        lo = i * sub
        prods, pieces = [], []
        for sl in range(sub):
            s = lo + sl
            for h in range(sub // tile):
                t0 = h * tile
                full = (t0 + tile - 1 <= sl) if rev else (t0 >= sl)
                skip = (t0 > sl) if rev else (t0 + tile - 1 < sl)
                if skip:
                    continue
                rows = slice(lo + t0, lo + t0 + tile)
                d = a[rows] - b[s:s + 1]
                if not full:
                    keep = (trow + t0 <= sl) if rev else (trow + t0 >= sl)
                    d = jnp.where(keep, d, NEG_INF)
                prods.append(q[rows] * jnp.exp2(d))
                pieces.append((h, sl))
        pr = jnp.concatenate(prods, axis=0).astype(BF16)
        sc = jnp.concatenate([_dot(pr[:, p], ones) for p in pairs], axis=1)
        acc = [None] * (sub // tile)
        for j, (h, sl) in enumerate(pieces):
            term = sc[j * tile:(j + 1) * tile] * v[lo + sl:lo + sl + 1]
            acc[h] = term if acc[h] is None else acc[h] + term
        o_i = jnp.concatenate(acc, axis=0)
        o_parts.append(o_i if o_far[i] is None else o_i + o_far[i])
    return jnp.concatenate(o_parts, axis=0) + o_state, st_new


def _gla_kernel(rev, final, *refs):
    if final:
        (q_ref, f_ref, v_ref, qc_ref, fc_ref, vc_ref, lbl_ref, l1m_ref,
         op_ref, g_ref, opc_ref, gc_ref, nw_ref, o_ref, oc_ref, st_ref) = refs
    else:
        (q_ref, f_ref, v_ref, qc_ref, fc_ref, vc_ref, lbl_ref, l1m_ref,
         o_ref, oc_ref, st_ref) = refs
    step = pl.program_id(1)

    def scan_block(qr, fr, vr, outr, prevr, gater):
        nchunk = qr.shape[1] // HG_CHUNK

        def body(j, carry):
            c = (nchunk - 1 - j) if rev else j
            rows = pl.ds(pl.multiple_of(c * HG_CHUNK, HG_CHUNK), HG_CHUNK)
            npair = HG_DIM // LANES
            log_f, log_k = _gla_gates(fr[0, rows, :], lbl_ref[...], l1m_ref[...])
            q = qr[0, rows, :] * (HEAD_DIM ** -0.5)
            o, st_new = _gla_chunk(q, log_k, log_f, vr[0, rows, :], [st_ref[p] for p in range(npair)], rev)
            for p in range(npair):
                st_ref[p] = st_new[p]
            if final:
                tot = o + prevr[0, rows, :]
                sq = tot * tot
                ms = jnp.concatenate([_dot_sel_lhs(sq[:, p * LANES:(p + 1) * LANES], _head_ones())
                                      for p in range(npair)], axis=1) * (1.0 / HEAD_DIM)
                o = tot * lax.rsqrt(ms + NORM_EPS) * nw_ref[...] * _silu(gater[0, rows, :])
            outr[0, rows, :] = o
            return carry

        lax.fori_loop(0, nchunk, body, 0)

    @pl.when(step == 0)
    def _():
        st_ref[...] = jnp.zeros_like(st_ref)
        scan_block(qc_ref, fc_ref, vc_ref, oc_ref, opc_ref if final else None, gc_ref if final else None)

    @pl.when(step > 0)
    def _():
        scan_block(q_ref, f_ref, v_ref, o_ref, op_ref if final else None, g_ref if final else None)


def _gla_scan(rest, rest_c, lbl, l1m, rev, prev=None, prev_c=None, norm_w=None, tb=512):
    b, t, _ = rest.shape
    ctx_len = rest_c.shape[1]
    nblk = t // tb
    final = prev is not None
    fkey = "f_bwd" if rev else "f_fwd"

    def blk(s):
        return (nblk - jnp.maximum(s, 1)) if rev else jnp.maximum(s - 1, 0)

    def lat(key):
        off = REST_HG_BLOCK[key]
        return pl.BlockSpec((1, tb, HG_DIM), lambda bi, s: (bi, blk(s), off))

    def ctx(key):
        off = REST_HG_BLOCK[key]
        return pl.BlockSpec((1, ctx_len, HG_DIM), lambda bi, s: (bi, 0, off))

    vec = pl.BlockSpec((1, HG_DIM), lambda bi, s: (0, 0))
    o_lat = pl.BlockSpec((1, tb, HG_DIM), lambda bi, s: (bi, blk(s), 0))
    o_ctx = pl.BlockSpec((1, ctx_len, HG_DIM), lambda bi, s: (bi, 0, 0))
    in_specs = [lat("q"), lat(fkey), lat("i"), ctx("q"), ctx(fkey), ctx("i"), vec, vec]
    args = [rest, rest, rest, rest_c, rest_c, rest_c, lbl, l1m]
    if final:
        in_specs += [o_lat, lat("g"), o_ctx, ctx("g"), vec]
        args += [prev, rest, prev_c, rest_c, norm_w]
    return pl.pallas_call(
        functools.partial(_gla_kernel, rev, final), name="gla_bwd" if rev else "gla_fwd",
        grid=(b, nblk + 1),
        in_specs=in_specs,
        out_specs=[o_lat, o_ctx],
        out_shape=[jax.ShapeDtypeStruct((b, t, HG_DIM), F32), jax.ShapeDtypeStruct((b, ctx_len, HG_DIM), F32)],
        scratch_shapes=[pltpu.VMEM((HG_DIM // LANES, LANES, LANES), F32)],
        compiler_params=_cparams(("arbitrary", "arbitrary")),
    )(*args)


HALO = 8


def _outproj_kernel(na_ref, hg_ref, cb_ref, cc_ref, cu_ref, ccp_ref, cup_ref, ccn_ref, cun_ref, x_ref,
                    w_ref, g_ref, sh_ref, sc_ref, nw_ref, cw_ref, wr_ref, xo_ref, h_ref, aff_ref):
    i = pl.program_id(1)
    last = pl.num_programs(1) - 1
    z = cc_ref[0] * cu_ref[0]
    tm = z.shape[0]
    row = lax.broadcasted_iota(jnp.int32, z.shape, 0)
    z_before = jnp.where(i > 0, ccp_ref[0, HALO - 1:HALO, :] * cup_ref[0, HALO - 1:HALO, :], 0.0)
    z_after = jnp.where(i < last, ccn_ref[0, 0:1, :] * cun_ref[0, 0:1, :], 0.0)
    zp = jnp.where(row == 0, z_before, pltpu.roll(z, 1, 0))
    zn = jnp.where(row == tm - 1, z_after, pltpu.roll(z, tm - 1, 0))
    cv = cb_ref[0] * (cw_ref[0:1, :] * zp + cw_ref[1:2, :] * z + cw_ref[2:3, :] * zn)
    mix = jnp.concatenate([na_ref[0], hg_ref[0], cv], axis=-1).astype(BF16)
    x = x_ref[0] + g_ref[0] * _dot(mix, w_ref[...])
    xo_ref[0] = x
    h = _rms_rows(x, nw_ref[...]) * (1.0 + sc_ref[0]) + sh_ref[0]
    h_ref[0] = h
    logits = _dot_nt(wr_ref[...], h.astype(BF16))
    p = jnp.exp(logits - logits.max(axis=0, keepdims=True))
    aff_ref[0] = p / p.sum(axis=0, keepdims=True)


def _outproj(na, hg, rest, x, w_out, gate, shift, scale, norm_w, conv_w, w_router_t, tm):
    b, t, d = x.shape
    nh = tm // HALO
    row = pl.BlockSpec((1, tm, d), lambda bi, i: (bi, i, 0))
    vec = pl.BlockSpec((1, 1, d), lambda bi, i: (bi, 0, 0))
    mixer = pl.BlockSpec((1, tm, NA_DIM), lambda bi, i: (bi, i, 0))

    def conv(c):
        return pl.BlockSpec((1, tm, CONV_DIM), lambda bi, i: (bi, i, c))

    def before(c):
        return pl.BlockSpec((1, HALO, CONV_DIM), lambda bi, i: (bi, jnp.maximum(i * nh - 1, 0), c))

    def after(c):
        return pl.BlockSpec((1, HALO, CONV_DIM), lambda bi, i: (bi, jnp.minimum((i + 1) * nh, t // HALO - 1), c))

    return pl.pallas_call(
        _outproj_kernel, name="outproj",
        grid=(b, t // tm),
        in_specs=[mixer, mixer, conv(0), conv(1), conv(2), before(1), before(2), after(1), after(2), row,
                  pl.BlockSpec(w_out.shape, lambda bi, i: (0, 0)), vec, vec, vec,
                  pl.BlockSpec((1, d), lambda bi, i: (0, 0)),
                  pl.BlockSpec(conv_w.shape, lambda bi, i: (0, 0)),
                  pl.BlockSpec(w_router_t.shape, lambda bi, i: (0, 0))],
        out_specs=[row, row, pl.BlockSpec((1, N_EXPERTS, tm), lambda bi, i: (bi, 0, i))],
        out_shape=[jax.ShapeDtypeStruct((b, t, d), F32), jax.ShapeDtypeStruct((b, t, d), F32),
                   jax.ShapeDtypeStruct((b, N_EXPERTS, t), F32)],
        compiler_params=_cparams(("arbitrary", "arbitrary")),
    )(na, hg, rest, rest, rest, rest, rest, rest, rest, x, w_out, gate, shift, scale, norm_w, conv_w, w_router_t)


def _route_kernel(cap, aff_ref, idx_ref, val_ref, off_ref, w_ref, tot_ref, roff_ref):
    ne, nr, _ = aff_ref.shape[1:]
    capp = idx_ref.shape[2]
    a = aff_ref[0]
    bits = pltpu.bitcast(a, jnp.int32)

    def count(m):
        return jnp.sum(jnp.sum(m.astype(F32), axis=2, keepdims=True), axis=1, keepdims=True)

    def search(k, thr):
        cand = thr | jnp.left_shift(jnp.int32(1), 30 - k)
        return jnp.where(count(bits >= cand) >= cap, cand, thr)

    thr = lax.fori_loop(0, 31, search, jnp.zeros((ne, 1, 1), jnp.int32))
    gt = bits > thr
    eq = bits == thr
    need = cap - count(gt)

    li = lax.broadcasted_iota(jnp.int32, (LANES, LANES), 0)
    lj = lax.broadcasted_iota(jnp.int32, (LANES, LANES), 1)
    before_lane = (li < lj).astype(BF16)
    upto_lane = (li <= lj).astype(BF16)
    all_lanes = jnp.ones((LANES, LANES), BF16)
    ri = lax.broadcasted_iota(jnp.int32, (nr, nr), 0)
    rj = lax.broadcasted_iota(jnp.int32, (nr, nr), 1)
    before_row = (rj < ri).astype(BF16)

    def prefix(m, lane_mat):
        m2 = m.astype(BF16).reshape(ne * nr, LANES)
        within = _dot(m2, lane_mat).reshape(ne, nr, LANES)
        tot = _dot(m2, all_lanes).reshape(ne, nr, LANES)
        roff = jnp.stack([_dot(before_row, tot[e].astype(BF16)) for e in range(ne)])
        return within, tot, roff

    w_eq, _, r_eq = prefix(eq, before_lane)
    sel = gt | (eq & (w_eq + r_eq < need))
    w_sel, t_sel, r_sel = prefix(sel, upto_lane)
    off_ref[0] = r_sel.astype(jnp.int32)
    w_ref[...] = w_sel
    tot_ref[...] = t_sel
    roff_ref[...] = r_sel

    slot = lax.broadcasted_iota(jnp.int32, (nr, capp), 1).astype(F32)
    rowid = lax.broadcasted_iota(jnp.int32, (nr, capp), 0).astype(F32)
    laneid = lax.broadcasted_iota(jnp.int32, (LANES, capp), 0).astype(F32)
    tn = (((0,), (0,)), ((), ()))

    def per_expert(e, carry):
        roff = roff_ref[e][:, 0:1]
        rincl = roff + tot_ref[e][:, 0:1]
        row_of = jnp.sum((rincl <= slot).astype(F32), axis=0, keepdims=True)
        onehot = rowid == row_of
        base = jnp.sum(jnp.where(onehot, roff, 0.0), axis=0, keepdims=True)
        onehot = onehot.astype(BF16)
        counts = lax.dot_general(w_ref[e].astype(BF16), onehot, tn, preferred_element_type=F32)
        col_of = jnp.sum((counts <= slot[0:1] - base).astype(F32), axis=0, keepdims=True)
        hi, mid, lo = _split3(aff_ref[0, e])
        vals = (lax.dot_general(hi, onehot, tn, preferred_element_type=F32)
                + lax.dot_general(mid, onehot, tn, preferred_element_type=F32)
                + lax.dot_general(lo, onehot, tn, preferred_element_type=F32))
        val = jnp.sum(jnp.where(laneid == col_of, vals, 0.0), axis=0, keepdims=True)
        live = slot[0:1] < cap
        idx_ref[0, pl.ds(e, 1), :] = jnp.where(live, row_of * LANES + col_of, 0.0).astype(jnp.int32)
        val_ref[0, pl.ds(e, 1), :] = jnp.where(live, val, 0.0)
        return carry

    lax.fori_loop(0, ne, per_expert, 0)


def _route(aff4, cap):
    b, ne, nr, _ = aff4.shape
    capp = -(-cap // LANES) * LANES
    blk = pl.BlockSpec((1, ne, nr, LANES), lambda bi: (bi, 0, 0, 0))
    lst = pl.BlockSpec((1, ne, capp), lambda bi: (bi, 0, 0))
    return pl.pallas_call(
        functools.partial(_route_kernel, cap), name="route",
        grid=(b,),
        in_specs=[blk],
        out_specs=[lst, lst, blk],
        out_shape=[jax.ShapeDtypeStruct((b, ne, capp), jnp.int32), jax.ShapeDtypeStruct((b, ne, capp), F32),
                   jax.ShapeDtypeStruct((b, ne, nr, LANES), jnp.int32)],
        scratch_shapes=[pltpu.VMEM((ne, nr, LANES), F32)] * 3,
        compiler_params=_cparams(("arbitrary",)),
    )(aff4)


MOE_SLOTS = 256
MOE_RANGE = 2048


def _moe_kernel(nrange, lo_ref, idx_ref, val_ref, h_ref, wg_ref, wu_ref, wd_ref, o_ref, xg_ref, yb_ref):
    bi, r, e = pl.program_id(0), pl.program_id(1), pl.program_id(2)
    rng = h_ref.shape[1]

    @pl.when(e == 0)
    def _():
        o_ref[...] = jnp.zeros_like(o_ref)

    base = (bi * pl.num_programs(2) + e) * (nrange + 1) + r
    lo, hi = lo_ref[base], lo_ref[base + 1]
    tok0 = r * rng

    def chunk(ci, carry):
        j0 = lo + ci * MOE_SLOTS
        cnt = jnp.minimum(MOE_SLOTS, hi - j0)
        xg_ref[...] = jnp.zeros_like(xg_ref)

        def gather(jj, c):
            tok = idx_ref[0, 0, j0 + jj] - tok0
            xg_ref[pl.ds(jj, 1), :] = h_ref[0, pl.ds(tok, 1), :]
            return c

        lax.fori_loop(0, cnt, gather, 0)
        xb = xg_ref[...].astype(BF16)
        hid = _silu(_dot(xb, wg_ref[0])) * _dot(xb, wu_ref[0])
        yb_ref[...] = _dot(hid.astype(BF16), wd_ref[0])

        def scatter(jj, c):
            tok = idx_ref[0, 0, j0 + jj] - tok0
            o_ref[0, pl.ds(tok, 1), :] += val_ref[0, 0, j0 + jj] * yb_ref[pl.ds(jj, 1), :]
            return c

        lax.fori_loop(0, cnt, scatter, 0)
        return carry

    lax.fori_loop(0, (hi - lo + MOE_SLOTS - 1) // MOE_SLOTS, chunk, 0)


def _moe(h2, idx, vals, off, cap, wg, wu, wd, rng):
    b, t, d = h2.shape
    ne, _, f = wg.shape
    nrange = t // rng
    capp = idx.shape[-1]
    lo = off[:, :, ::rng // LANES, 0][:, :, :nrange]
    lo = jnp.concatenate([lo, jnp.full((b, ne, 1), cap, jnp.int32)], axis=-1).reshape(-1)
    lst = pl.BlockSpec((1, 1, capp), lambda bi, r, e, lo_ref: (bi * ne + e, 0, 0), memory_space=pltpu.SMEM)
    tile = pl.BlockSpec((1, rng, d), lambda bi, r, e, lo_ref: (bi, r, 0))
    return pl.pallas_call(
        functools.partial(_moe_kernel, nrange), name="moe",
        grid_spec=pltpu.PrefetchScalarGridSpec(
            num_scalar_prefetch=1,
            grid=(b, nrange, ne),
            in_specs=[lst, lst, tile,
                      pl.BlockSpec((1, d, f), lambda bi, r, e, lo_ref: (e, 0, 0)),
                      pl.BlockSpec((1, d, f), lambda bi, r, e, lo_ref: (e, 0, 0)),
                      pl.BlockSpec((1, f, d), lambda bi, r, e, lo_ref: (e, 0, 0))],
            out_specs=tile,
            scratch_shapes=[pltpu.VMEM((MOE_SLOTS, d), F32), pltpu.VMEM((MOE_SLOTS, d), F32)]),
        out_shape=jax.ShapeDtypeStruct((b, t, d), F32),
        compiler_params=_cparams(("arbitrary", "arbitrary", "arbitrary")),
    )(lo, idx.reshape(b * ne, 1, capp), vals.reshape(b * ne, 1, capp), h2, wg, wu, wd)


def _residual_kernel(x_ref, y_ref, g_ref, o_ref):
    o_ref[0] = x_ref[0] + g_ref[0] * y_ref[0]


def _residual(x, y, gate, tm):
    b, t, d = x.shape
    row = pl.BlockSpec((1, tm, d), lambda bi, i: (bi, i, 0))
    return pl.pallas_call(
        _residual_kernel, name="residual", grid=(b, t // tm),
        in_specs=[row, row, pl.BlockSpec((1, 1, d), lambda bi, i: (bi, 0, 0))],
        out_specs=row, out_shape=jax.ShapeDtypeStruct((b, t, d), F32),
        compiler_params=_cparams(("arbitrary", "arbitrary")),
    )(x, y, gate)


def _hgrn_lower_bounds(logits):
    p = jax.nn.softmax(logits.astype(F32), axis=1)
    return jnp.concatenate([jnp.zeros_like(p[:, :1]), jnp.cumsum(p[:, 1:], axis=1)], axis=1)


def _expert_choice_ffn(h2, aff, wg, wu, wd, rng):
    b, ne, t = aff.shape
    cap = EC_CAPACITY_FACTOR * t // ne
    rows = max(t // LANES, 8)
    pad = rows * LANES - t
    if pad:
        aff = jnp.pad(aff, ((0, 0), (0, 0), (0, pad)), constant_values=-1.0)
    idx, vals, off = _route(aff.reshape(b, ne, rows, LANES), cap)
    return _moe(h2, idx, vals, off, cap, wg, wu, wd, rng)


def kernel(x, c, ctx, c_ctx, w_mod, b_mod, norm1_w, w_in, na_q_norm, na_k_norm, na_rpb, hg_lb_logits, hg_norm,
           conv_w, w_out, norm2_w, w_router, w_exp_gate, w_exp_up, w_exp_down):
    b, t, d = x.shape
    wg, wu, wd = w_exp_gate.astype(BF16), w_exp_up.astype(BF16), w_exp_down.astype(BF16)
    depth = w_mod.shape[0]
    ctx_len = ctx.shape[1]
    cond = jnp.zeros((8, d), F32).at[0:b].set(c).at[b].set(c_ctx)
    mod = _modvec(cond, w_mod, b_mod).reshape(depth, 8, N_MOD, d)
    cos, sa, sb = _rope_tables(t)
    cos_c = jnp.ones((ctx_len, LANES), F32)
    zero_c = jnp.zeros((ctx_len, LANES), F32)
    lb = _hgrn_lower_bounds(hg_lb_logits)
    w_in_p = _permute_w_in(w_in)
    w_out_b = w_out.astype(BF16)
    w_router_t = w_router.transpose(0, 2, 1).astype(BF16)
    bias = jax.vmap(lambda r: _na_bias_tiles(r, t // GRID_W))(na_rpb)
    tile2 = lambda v: jnp.tile(v, LANES // HEAD_DIM)[None]

    cx = ctx
    y = y_c = None
    for l in range(depth):
        need_ctx = l < depth - 1
        m_lat = [mod[l, 0:b, j][:, None, :] for j in range(N_MOD)]
        m_ctx = [jnp.broadcast_to(mod[l, b, j][None, None, :], (b, 1, d)) for j in range(N_MOD)]
        g_prev = [mod[l - 1, 0:b, 5][:, None, :], jnp.broadcast_to(mod[l - 1, b, 5][None, None, :], (b, 1, d))] if l else [None, None]
        qn, kn = tile2(na_q_norm[l]), tile2(na_k_norm[l])
        x, qkv, rest = _inproj(x, y, g_prev[0], m_lat[0], m_lat[1], norm1_w[l][None], w_in_p[l], cos, sa, sb, qn, kn, 512)
        cx, qkv_c, rest_c = _inproj(cx, y_c, g_prev[1], m_ctx[0], m_ctx[1], norm1_w[l][None], w_in_p[l],
                                    cos_c, zero_c, zero_c, qn, kn, ctx_len)
        na = _na_attention(qkv, qkv_c, bias[l])
        o = o_c = None
        for rev in (False, True):
            lbd = lb[1 if rev else 0, l][None]
            o, o_c = _gla_scan(rest, rest_c, jnp.log(lbd), jnp.log1p(-lbd), rev, o, o_c,
                               jnp.tile(hg_norm[l], HG_HEADS)[None] if rev else None)
        x, h2, aff = _outproj(na, o, rest, x, w_out_b[l], m_lat[2], m_lat[3], m_lat[4], norm2_w[l][None],
                              conv_w[l], w_router_t[l], 512)
        y = _expert_choice_ffn(h2, aff, wg[l], wu[l], wd[l], min(MOE_RANGE, t))
        if need_ctx:
            na_c = _ctx_attention(qkv_c)
            cx, h2_c, aff_c = _outproj(na_c, o_c, rest_c, cx, w_out_b[l], m_ctx[2], m_ctx[3], m_ctx[4],
                                       norm2_w[l][None], conv_w[l], w_router_t[l], ctx_len)
            y_c = _expert_choice_ffn(h2_c, aff_c, wg[l], wu[l], wd[l], ctx_len)
    return _residual(x, y, mod[depth - 1, 0:b, 5][:, None, :], 512)
```

```python
import functools

import jax
import jax.numpy as jnp
import numpy as np
from jax import lax
from jax.experimental import pallas as pl
from jax.experimental.pallas import tpu as pltpu

GRID_W = 64
HEAD_DIM = 64
NA_HEADS = 6
NA_DIM = NA_HEADS * HEAD_DIM
NA_ROWS = 8
NA_COLS = 16
HG_HEADS = 6
HG_DIM = HG_HEADS * HEAD_DIM
HG_CHUNK = 64
HG_SUB = 16
CONV_DIM = 256
N_EXPERTS = 16
EC_CAPACITY_FACTOR = 2
ROPE_BASE = 10000.0
NORM_EPS = 1e-6
N_MOD = 6

LANES = 128
NA_QROWS = 4
NA_KROWS = NA_QROWS + NA_ROWS - 1
VMEM_LIMIT = 56 * 1024 * 1024

F32 = jnp.float32
BF16 = jnp.bfloat16
NEG_INF = float("-inf")


def _cparams(sem):
    return pltpu.CompilerParams(dimension_semantics=sem, vmem_limit_bytes=VMEM_LIMIT)


def _dot(a, b):
    return jnp.dot(a, b, preferred_element_type=F32)


def _dot_nt(a, b):
    return lax.dot_general(a, b, (((1,), (1,)), ((), ())), preferred_element_type=F32)


def _split3(x):
    hi = x.astype(BF16)
    r1 = x - hi.astype(F32)
    mid = r1.astype(BF16)
    lo = (r1 - mid.astype(F32)).astype(BF16)
    return hi, mid, lo


def _dot_sel_rhs(sel, x):
    hi, mid, lo = _split3(x)
    return _dot(sel, hi) + _dot(sel, mid) + _dot(sel, lo)


def _dot_sel_lhs(x, sel):
    hi, mid, lo = _split3(x)
    return _dot(hi, sel) + _dot(mid, sel) + _dot(lo, sel)


def _head_ones():
    r = lax.broadcasted_iota(jnp.int32, (LANES, LANES), 0) // HEAD_DIM
    c = lax.broadcasted_iota(jnp.int32, (LANES, LANES), 1) // HEAD_DIM
    return (r == c).astype(BF16)


def _silu(x):
    return x * (1.0 / (1.0 + jnp.exp(-x)))


def _load_row_tiles(ref, lead, nrows):
    nchunk = ref.shape[-2] // nrows
    return jnp.concatenate([ref[lead + (pl.ds(c, nrows, stride=nchunk), slice(None))] for c in range(nchunk)],
                           axis=1)


def _store_row_tiles(ref, lead, x):
    nrows = x.shape[0]
    nchunk = x.shape[1] // LANES
    for c in range(nchunk):
        ref[lead + (pl.ds(c, nrows, stride=nchunk), slice(None))] = x[:, c * LANES:(c + 1) * LANES]


def _modvec_kernel(cond_ref, w_ref, b_ref, o_ref):
    a = _silu(cond_ref[...]).astype(BF16)
    o_ref[0] = _dot(a, w_ref[0].astype(BF16)) + b_ref[0]


def _modvec(cond, w_mod, b_mod):
    depth, d, n = w_mod.shape
    tn = 512
    return pl.pallas_call(
        _modvec_kernel, name="modvec",
        grid=(depth, n // tn),
        in_specs=[
            pl.BlockSpec((8, d), lambda l, j: (0, 0)),
            pl.BlockSpec((1, d, tn), lambda l, j: (l, 0, j)),
            pl.BlockSpec((1, 1, tn), lambda l, j: (l, 0, j)),
        ],
        out_specs=pl.BlockSpec((1, 8, tn), lambda l, j: (l, 0, j)),
        out_shape=jax.ShapeDtypeStruct((depth, 8, n), F32),
        compiler_params=_cparams(("arbitrary", "arbitrary")),
    )(cond, w_mod, b_mod.reshape(depth, 1, n))


QKV_COLS = 3 * NA_DIM
REST_COLS = 3 * CONV_DIM + 5 * HG_DIM


def _rms_rows(x, w):
    ms = jnp.mean(x * x, axis=-1, keepdims=True)
    return x * lax.rsqrt(ms + NORM_EPS) * w


def _inproj_kernel(has_y, *refs):
    if has_y:
        (x_ref, y_ref, g_ref, sh_ref, sc_ref, nw_ref, w_ref, cos_ref, sa_ref, sb_ref, qn_ref, kn_ref,
         xo_ref, qkv_ref, rest_ref) = refs
        x = x_ref[0] + g_ref[0] * _load_row_tiles(y_ref, (0,), x_ref.shape[1])
        xo_ref[0] = x
    else:
        (x_ref, sh_ref, sc_ref, nw_ref, w_ref, cos_ref, sa_ref, sb_ref, qn_ref, kn_ref,
         qkv_ref, rest_ref) = refs
        x = x_ref[0]
    h = _rms_rows(x, nw_ref[...]) * (1.0 + sc_ref[0]) + sh_ref[0]
    hb = h.astype(BF16)
    ones = _head_ones()
    cos = cos_ref[...]
    sa = sa_ref[...]
    sb = sb_ref[...]
    for c in range(2 * NA_DIM // LANES):
        p = _dot(hb, w_ref[:, c * LANES:(c + 1) * LANES])
        ssq = _dot_sel_lhs(p * p, ones)
        nw = qn_ref[...] if c < NA_DIM // LANES else kn_ref[...]
        pn = p * lax.rsqrt(ssq * (1.0 / HEAD_DIM) + NORM_EPS) * nw
        pr = pn * cos + pltpu.roll(pn, LANES - 16, 1) * sa + pltpu.roll(pn, 16, 1) * sb
        if c < NA_DIM // LANES:
            pr = pr * (HEAD_DIM ** -0.5)
        qkv_ref[0, :, c * LANES:(c + 1) * LANES] = pr.astype(BF16)
    qkv_ref[0, :, 2 * NA_DIM:] = _dot(hb, w_ref[:, 2 * NA_DIM:QKV_COLS]).astype(BF16)
    for c in range(REST_COLS // HG_DIM):
        lo = QKV_COLS + c * HG_DIM
        rest_ref[0, :, c * HG_DIM:(c + 1) * HG_DIM] = _dot(hb, w_ref[:, lo:lo + HG_DIM])


def _inproj(x, y, gate, shift, scale, norm_w, w_in, cos, sa, sb, qn, kn, tm):
    b, t, d = x.shape
    has_y = y is not None
    row = pl.BlockSpec((1, tm, d), lambda bi, i: (bi, i, 0))
    vec = pl.BlockSpec((1, 1, d), lambda bi, i: (bi, 0, 0))
    tab = pl.BlockSpec((tm, LANES), lambda bi, i: (i, 0))
    small = pl.BlockSpec((1, LANES), lambda bi, i: (0, 0))
    row_tiles = pl.BlockSpec((1, tm * d // LANES, LANES), lambda bi, i: (bi, i, 0))
    in_specs = [row] + ([row_tiles, vec] if has_y else []) + [
        vec, vec, pl.BlockSpec((1, d), lambda bi, i: (0, 0)),
        pl.BlockSpec(w_in.shape, lambda bi, i: (0, 0)), tab, tab, tab, small, small]
    out_specs = ([row] if has_y else []) + [
        pl.BlockSpec((1, tm, QKV_COLS), lambda bi, i: (bi, i, 0)),
        pl.BlockSpec((1, tm, REST_COLS), lambda bi, i: (bi, i, 0))]
    out_shape = ([jax.ShapeDtypeStruct((b, t, d), F32)] if has_y else []) + [
        jax.ShapeDtypeStruct((b, t, QKV_COLS), BF16), jax.ShapeDtypeStruct((b, t, REST_COLS), F32)]
    args = [x] + ([y, gate] if has_y else []) + [shift, scale, norm_w, w_in, cos, sa, sb, qn, kn]
    outs = pl.pallas_call(
        functools.partial(_inproj_kernel, has_y), name="inproj",
        grid=(b, t // tm), in_specs=in_specs, out_specs=out_specs, out_shape=out_shape,
        compiler_params=_cparams(("arbitrary", "arbitrary")),
    )(*args)
    return outs if has_y else [x] + list(outs)


def _rope_tables(n):
    t = jnp.arange(n, dtype=jnp.int32)
    row = (t // GRID_W).astype(F32)
    col = (t % GRID_W).astype(F32)
    n_freq = HEAD_DIM // 4
    inv = ROPE_BASE ** (-jnp.arange(n_freq, dtype=F32) / n_freq)
    ar = row[:, None] * inv
    ac = col[:, None] * inv
    ang = jnp.concatenate([ar, ar, ac, ac], axis=-1)
    cos = jnp.tile(jnp.cos(ang), (1, LANES // HEAD_DIM))
    sin = jnp.tile(jnp.sin(ang), (1, LANES // HEAD_DIM))
    quarter = (np.arange(LANES) // 16) % 2
    sa = sin * jnp.asarray(np.where(quarter == 0, -1.0, 0.0), F32)
    sb = sin * jnp.asarray(np.where(quarter == 1, 1.0, 0.0), F32)
    return cos, sa, sb


def _permute_w_in(w):
    qkv = w[..., :QKV_COLS]
    hg = w[..., QKV_COLS:QKV_COLS + 5 * HG_DIM]
    cv = w[..., QKV_COLS + 5 * HG_DIM:]
    return jnp.concatenate([qkv, cv, hg], axis=-1).astype(BF16)


def _na_bias_tiles(rpb, rows):
    nh, n_rel = rpb.shape[0], 2 * NA_ROWS - 1
    qc = np.arange(GRID_W)[:, None]
    kc = np.arange(GRID_W)[None, :]
    ws = np.clip(qc - NA_COLS // 2, 0, GRID_W - NA_COLS)
    col_ok = (kc >= ws) & (kc < ws + NA_COLS)
    rel_col = np.clip(kc - qc + NA_COLS - 1, 0, 2 * NA_COLS - 2)
    blocks = jnp.where(jnp.asarray(col_ok), rpb[:, :, rel_col], NEG_INF)
    blocks = jnp.concatenate([blocks, jnp.full((nh, 1, GRID_W, GRID_W), NEG_INF, rpb.dtype)], axis=1)
    qi = np.arange(NA_QROWS)[:, None]
    kj = np.arange(NA_KROWS)[None, :]
    which = []
    for r0 in (0, NA_QROWS, rows - NA_QROWS):
        kb = int(np.clip(r0 - NA_ROWS // 2, 0, rows - NA_KROWS))
        qr, kr = r0 + qi, kb + kj
        rs = np.clip(qr - NA_ROWS // 2, 0, rows - NA_ROWS)
        ok = (kr >= rs) & (kr < rs + NA_ROWS)
        which.append(np.where(ok, kr - qr + NA_ROWS - 1, n_rel))
    which = np.stack(which).astype(np.int32)
    tiles = blocks[:, which]
    return tiles.transpose(1, 0, 2, 4, 3, 5).reshape(3, nh, NA_QROWS * GRID_W, NA_KROWS * GRID_W)


def _softmax_pv(scores, values):
    m = scores[0].max(axis=-1, keepdims=True)
    for s in scores[1:]:
        m = jnp.maximum(m, s.max(axis=-1, keepdims=True))
    acc, den = None, None
    for s, v in zip(scores, values):
        p = jnp.exp(s - m)
        d = p.sum(axis=-1, keepdims=True)
        o = _dot(p.astype(BF16), v)
        acc = o if acc is None else acc + o
        den = d if den is None else den + d
    return acc / den


def _na_kernel(rows, q_ref, k_ref, v_ref, kc_ref, vc_ref, bias_ref, o_ref):
    i = pl.program_id(2)
    kb = jnp.clip(i * NA_QROWS - NA_ROWS // 2, 0, rows - NA_KROWS)
    start = pl.multiple_of(kb * GRID_W, GRID_W)
    q = q_ref[0]
    kt = k_ref[0, pl.ds(start, NA_KROWS * GRID_W), :]
    vt = v_ref[0, pl.ds(start, NA_KROWS * GRID_W), :]
    kc = kc_ref[0]
    vc = vc_ref[0]
    lane = lax.broadcasted_iota(jnp.int32, q.shape, 1)
    outs = []
    for hh in range(LANES // HEAD_DIM):
        qh = jnp.where((lane // HEAD_DIM) == hh, q, jnp.zeros_like(q))
        s_loc = _dot_nt(qh, kt) + bias_ref[0, hh]
        s_ctx = _dot_nt(qh, kc)
        outs.append(_softmax_pv([s_loc, s_ctx], [vt, vc]))
    o_ref[0] = jnp.where(lane < HEAD_DIM, outs[0], outs[1])


def _na_attention(qkv, qkv_c, bias):
    b, t, _ = qkv.shape
    ctx_len = qkv_c.shape[1]
    rows = t // GRID_W
    nq, nk = NA_QROWS * GRID_W, NA_KROWS * GRID_W
    nt = t // nq
    nhp = NA_DIM // LANES

    def variant(i):
        return jnp.where(i == 0, 0, jnp.where(i == nt - 1, 2, 1))

    return pl.pallas_call(
        functools.partial(_na_kernel, rows), name="na_attn",
        grid=(b, nhp, nt),
        in_specs=[
            pl.BlockSpec((1, nq, LANES), lambda bi, hp, i: (bi, i, hp)),
            pl.BlockSpec((1, t, LANES), lambda bi, hp, i: (bi, 0, nhp + hp)),
            pl.BlockSpec((1, t, LANES), lambda bi, hp, i: (bi, 0, 2 * nhp + hp)),
            pl.BlockSpec((1, ctx_len, LANES), lambda bi, hp, i: (bi, 0, nhp + hp)),
            pl.BlockSpec((1, ctx_len, LANES), lambda bi, hp, i: (bi, 0, 2 * nhp + hp)),
            pl.BlockSpec((1, LANES // HEAD_DIM, nq, nk), lambda bi, hp, i: (variant(i), hp, 0, 0)),
        ],
        out_specs=pl.BlockSpec((1, nq, LANES), lambda bi, hp, i: (bi, i, hp)),
        out_shape=jax.ShapeDtypeStruct((b, t, NA_DIM), F32),
        compiler_params=_cparams(("arbitrary", "arbitrary", "arbitrary")),
    )(qkv, qkv, qkv, qkv_c, qkv_c, bias)


def _ctx_attn_kernel(q_ref, k_ref, v_ref, o_ref):
    q = q_ref[0]
    lane = lax.broadcasted_iota(jnp.int32, q.shape, 1)
    outs = []
    for hh in range(LANES // HEAD_DIM):
        qh = jnp.where((lane // HEAD_DIM) == hh, q, jnp.zeros_like(q))
        outs.append(_softmax_pv([_dot_nt(qh, k_ref[0])], [v_ref[0]]))
    o_ref[0] = jnp.where(lane < HEAD_DIM, outs[0], outs[1])


def _ctx_attention(qkv_c):
    b, n, _ = qkv_c.shape
    nhp = NA_DIM // LANES
    return pl.pallas_call(
        _ctx_attn_kernel, name="ctx_attn",
        grid=(b, nhp),
        in_specs=[pl.BlockSpec((1, n, LANES), lambda bi, hp: (bi, 0, hp)),
                  pl.BlockSpec((1, n, LANES), lambda bi, hp: (bi, 0, nhp + hp)),
                  pl.BlockSpec((1, n, LANES), lambda bi, hp: (bi, 0, 2 * nhp + hp))],
        out_specs=pl.BlockSpec((1, n, LANES), lambda bi, hp: (bi, 0, hp)),
        out_shape=jax.ShapeDtypeStruct((b, n, NA_DIM), F32),
        compiler_params=_cparams(("arbitrary", "arbitrary")),
    )(qkv_c, qkv_c, qkv_c)


REST_HG_BLOCK = {"q": 2, "f_fwd": 3, "f_bwd": 4, "i": 5, "g": 6}


LOG2E = 1.4426950408889634


def _gla_gates(z, lbl, l1m):
    soft = jnp.log1p(jnp.exp(-jnp.abs(z)))
    y = l1m + (jnp.minimum(z, 0.0) - soft)
    hi = jnp.maximum(lbl, y)
    lo = jnp.minimum(lbl, y)
    log_f = hi + jnp.log1p(jnp.exp(lo - hi))
    log_k = l1m + (jnp.minimum(-z, 0.0) - soft)
    return log_f * LOG2E, log_k * LOG2E


def _gla_chunk(q, lk, g, v, st, rev):
    n, sub, tile = HG_CHUNK, HG_SUB, 8
    width = q.shape[1]
    nheads = width // HEAD_DIM
    pairs = [slice(p * LANES, (p + 1) * LANES) for p in range(width // LANES)]
    head_of_lane = lax.broadcasted_iota(jnp.int32, (sub, width), 1) // HEAD_DIM
    trow = lax.broadcasted_iota(jnp.int32, (tile, width), 0)
    ti = lax.broadcasted_iota(jnp.int32, (n, n), 0)
    si = lax.broadcasted_iota(jnp.int32, (n, n), 1)
    tri = ((ti <= si) if rev else (ti >= si)).astype(BF16)
    a = _dot_sel_rhs(tri, g)
    b = a - lk
    vb = v.astype(BF16)

    qe = (q * jnp.exp2(a)).astype(BF16)
    o_state = jnp.concatenate([_dot_nt(qe[:, p], st[j].astype(BF16)) for j, p in enumerate(pairs)], axis=1)
    a_end = a[0:1] if rev else a[n - 1:n]
    kd_end = jnp.exp2(a_end - b).astype(BF16)
    decay = jnp.exp2(a_end)
    bi = lax.broadcasted_iota(jnp.int32, (LANES, LANES), 0) // HEAD_DIM
    bj = lax.broadcasted_iota(jnp.int32, (LANES, LANES), 1) // HEAD_DIM
    tn = (((0,), (0,)), ((), ()))
    st_new = [st[j] * decay[:, p] + jnp.where(bi == bj, lax.dot_general(vb[:, p], kd_end[:, p], tn,
                                                                        preferred_element_type=F32), 0.0)
              for j, p in enumerate(pairs)]

    o_far = []
    for i in range(n // sub):
        lo = i * sub
        other = slice(lo + sub, n) if rev else slice(0, lo)
        if other.stop == other.start:
            o_far.append(None)
            continue
        r = a[lo + sub - 1:lo + sub] if rev else a[lo:lo + 1]
        qd = q[lo:lo + sub] * jnp.exp2(a[lo:lo + sub] - r)
        kd = jnp.exp2(r - b[other]).astype(BF16)
        stacked = jnp.concatenate([jnp.where(head_of_lane == h, qd, 0.0) for h in range(nheads)], axis=0)
        o_all = _dot(_dot_nt(stacked.astype(BF16), kd).astype(BF16), vb[other])
        pick = o_all[0:sub]
        for h in range(1, nheads):
            pick = jnp.where(head_of_lane == h, o_all[h * sub:(h + 1) * sub], pick)
        o_far.append(pick)

    ones = _head_ones()
    o_parts = []
    for i in range(n // sub):
        lo = i * sub
        prods, pieces = [], []
        for sl in range(sub):
            s = lo + sl
            for h in range(sub // tile):
                t0 = h * tile
                full = (t0 + tile - 1 <= sl) if rev else (t0 >= sl)
                skip = (t0 > sl) if rev else (t0 + tile - 1 < sl)
                if skip:
                    continue
                rows = slice(lo + t0, lo + t0 + tile)
                d = a[rows] - b[s:s + 1]
                if not full:
                    keep = (trow + t0 <= sl) if rev else (trow + t0 >= sl)
                    d = jnp.where(keep, d, NEG_INF)
                prods.append(q[rows] * jnp.exp2(d))
                pieces.append((h, sl))
        pr = jnp.concatenate(prods, axis=0).astype(BF16)
        sc = jnp.concatenate([_dot(pr[:, p], ones) for p in pairs], axis=1)
        acc = [None] * (sub // tile)
        for j, (h, sl) in enumerate(pieces):
            term = sc[j * tile:(j + 1) * tile] * v[lo + sl:lo + sl + 1]
            acc[h] = term if acc[h] is None else acc[h] + term
        o_i = jnp.concatenate(acc, axis=0)
        o_parts.append(o_i if o_far[i] is None else o_i + o_far[i])
    return jnp.concatenate(o_parts, axis=0) + o_state, st_new


def _gla_kernel(rev, final, *refs):
    if final:
        (q_ref, f_ref, v_ref, qc_ref, fc_ref, vc_ref, lbl_ref, l1m_ref,
         op_ref, g_ref, opc_ref, gc_ref, nw_ref, o_ref, oc_ref, st_ref) = refs
    else:
        (q_ref, f_ref, v_ref, qc_ref, fc_ref, vc_ref, lbl_ref, l1m_ref,
         o_ref, oc_ref, st_ref) = refs
    step = pl.program_id(1)

    def scan_block(qr, fr, vr, outr, prevr, gater):
        nchunk = qr.shape[1] // HG_CHUNK

        def body(j, carry):
            c = (nchunk - 1 - j) if rev else j
            rows = pl.ds(pl.multiple_of(c * HG_CHUNK, HG_CHUNK), HG_CHUNK)
            npair = HG_DIM // LANES
            log_f, log_k = _gla_gates(fr[0, rows, :], lbl_ref[...], l1m_ref[...])
            q = qr[0, rows, :] * (HEAD_DIM ** -0.5)
            o, st_new = _gla_chunk(q, log_k, log_f, vr[0, rows, :], [st_ref[p] for p in range(npair)], rev)
            for p in range(npair):
                st_ref[p] = st_new[p]
            if final:
                tot = o + prevr[0, rows, :]
                sq = tot * tot
                ms = jnp.concatenate([_dot_sel_lhs(sq[:, p * LANES:(p + 1) * LANES], _head_ones())
                                      for p in range(npair)], axis=1) * (1.0 / HEAD_DIM)
                o = tot * lax.rsqrt(ms + NORM_EPS) * nw_ref[...] * _silu(gater[0, rows, :])
            outr[0, rows, :] = o
            return carry

        lax.fori_loop(0, nchunk, body, 0)

    @pl.when(step == 0)
    def _():
        st_ref[...] = jnp.zeros_like(st_ref)
        scan_block(qc_ref, fc_ref, vc_ref, oc_ref, opc_ref if final else None, gc_ref if final else None)

    @pl.when(step > 0)
    def _():
        scan_block(q_ref, f_ref, v_ref, o_ref, op_ref if final else None, g_ref if final else None)


def _gla_scan(rest, rest_c, lbl, l1m, rev, prev=None, prev_c=None, norm_w=None, tb=512):
    b, t, _ = rest.shape
    ctx_len = rest_c.shape[1]
    nblk = t // tb
    final = prev is not None
    fkey = "f_bwd" if rev else "f_fwd"

    def blk(s):
        return (nblk - jnp.maximum(s, 1)) if rev else jnp.maximum(s - 1, 0)

    def lat(key):
        off = REST_HG_BLOCK[key]
        return pl.BlockSpec((1, tb, HG_DIM), lambda bi, s: (bi, blk(s), off))

    def ctx(key):
        off = REST_HG_BLOCK[key]
        return pl.BlockSpec((1, ctx_len, HG_DIM), lambda bi, s: (bi, 0, off))

    vec = pl.BlockSpec((1, HG_DIM), lambda bi, s: (0, 0))
    o_lat = pl.BlockSpec((1, tb, HG_DIM), lambda bi, s: (bi, blk(s), 0))
    o_ctx = pl.BlockSpec((1, ctx_len, HG_DIM), lambda bi, s: (bi, 0, 0))
    in_specs = [lat("q"), lat(fkey), lat("i"), ctx("q"), ctx(fkey), ctx("i"), vec, vec]
    args = [rest, rest, rest, rest_c, rest_c, rest_c, lbl, l1m]
    if final:
        in_specs += [o_lat, lat("g"), o_ctx, ctx("g"), vec]
        args += [prev, rest, prev_c, rest_c, norm_w]
    return pl.pallas_call(
        functools.partial(_gla_kernel, rev, final), name="gla_bwd" if rev else "gla_fwd",
        grid=(b, nblk + 1),
        in_specs=in_specs,
        out_specs=[o_lat, o_ctx],
        out_shape=[jax.ShapeDtypeStruct((b, t, HG_DIM), F32), jax.ShapeDtypeStruct((b, ctx_len, HG_DIM), F32)],
        scratch_shapes=[pltpu.VMEM((HG_DIM // LANES, LANES, LANES), F32)],
        compiler_params=_cparams(("arbitrary", "arbitrary")),
    )(*args)


HALO = 8


def _outproj_kernel(na_ref, hg_ref, cb_ref, cc_ref, cu_ref, ccp_ref, cup_ref, ccn_ref, cun_ref, x_ref,
                    w_ref, g_ref, sh_ref, sc_ref, nw_ref, cw_ref, wr_ref, xo_ref, h_ref, aff_ref):
    i = pl.program_id(1)
    last = pl.num_programs(1) - 1
    z = cc_ref[0] * cu_ref[0]
    tm = z.shape[0]
    row = lax.broadcasted_iota(jnp.int32, z.shape, 0)
    z_before = jnp.where(i > 0, ccp_ref[0, HALO - 1:HALO, :] * cup_ref[0, HALO - 1:HALO, :], 0.0)
    z_after = jnp.where(i < last, ccn_ref[0, 0:1, :] * cun_ref[0, 0:1, :], 0.0)
    zp = jnp.where(row == 0, z_before, pltpu.roll(z, 1, 0))
    zn = jnp.where(row == tm - 1, z_after, pltpu.roll(z, tm - 1, 0))
    cv = cb_ref[0] * (cw_ref[0:1, :] * zp + cw_ref[1:2, :] * z + cw_ref[2:3, :] * zn)
    mix = jnp.concatenate([na_ref[0], hg_ref[0], cv], axis=-1).astype(BF16)
    x = x_ref[0] + g_ref[0] * _dot(mix, w_ref[...])
    xo_ref[0] = x
    h = _rms_rows(x, nw_ref[...]) * (1.0 + sc_ref[0]) + sh_ref[0]
    _store_row_tiles(h_ref, (0,), h)
    logits = _dot_nt(wr_ref[...], h.astype(BF16))
    p = jnp.exp(logits - logits.max(axis=0, keepdims=True))
    aff_ref[0] = p / p.sum(axis=0, keepdims=True)


def _outproj(na, hg, rest, x, w_out, gate, shift, scale, norm_w, conv_w, w_router_t, tm):
    b, t, d = x.shape
    nh = tm // HALO
    row = pl.BlockSpec((1, tm, d), lambda bi, i: (bi, i, 0))
    vec = pl.BlockSpec((1, 1, d), lambda bi, i: (bi, 0, 0))
    mixer = pl.BlockSpec((1, tm, NA_DIM), lambda bi, i: (bi, i, 0))

    def conv(c):
        return pl.BlockSpec((1, tm, CONV_DIM), lambda bi, i: (bi, i, c))

    def before(c):
        return pl.BlockSpec((1, HALO, CONV_DIM), lambda bi, i: (bi, jnp.maximum(i * nh - 1, 0), c))

    def after(c):
        return pl.BlockSpec((1, HALO, CONV_DIM), lambda bi, i: (bi, jnp.minimum((i + 1) * nh, t // HALO - 1), c))

    return pl.pallas_call(
        _outproj_kernel, name="outproj",
        grid=(b, t // tm),
        in_specs=[mixer, mixer, conv(0), conv(1), conv(2), before(1), before(2), after(1), after(2), row,
                  pl.BlockSpec(w_out.shape, lambda bi, i: (0, 0)), vec, vec, vec,
                  pl.BlockSpec((1, d), lambda bi, i: (0, 0)),
                  pl.BlockSpec(conv_w.shape, lambda bi, i: (0, 0)),
                  pl.BlockSpec(w_router_t.shape, lambda bi, i: (0, 0))],
        out_specs=[row, pl.BlockSpec((1, tm * d // LANES, LANES), lambda bi, i: (bi, i, 0)),
                   pl.BlockSpec((1, N_EXPERTS, tm), lambda bi, i: (bi, 0, i))],
        out_shape=[jax.ShapeDtypeStruct((b, t, d), F32), jax.ShapeDtypeStruct((b, t * d // LANES, LANES), F32),
                   jax.ShapeDtypeStruct((b, N_EXPERTS, t), F32)],
        compiler_params=_cparams(("arbitrary", "arbitrary")),
    )(na, hg, rest, rest, rest, rest, rest, rest, rest, x, w_out, gate, shift, scale, norm_w, conv_w, w_router_t)


def _route_kernel(cap, aff_ref, idx_ref, val_ref, off_ref, w_ref, tot_ref, roff_ref):
    ne, nr, _ = aff_ref.shape[1:]
    capp = idx_ref.shape[2]
    a = aff_ref[0]
    bits = pltpu.bitcast(a, jnp.int32)

    def count(m):
        return jnp.sum(jnp.sum(m.astype(F32), axis=2, keepdims=True), axis=1, keepdims=True)

    def search(k, thr):
        cand = thr | jnp.left_shift(jnp.int32(1), 30 - k)
        return jnp.where(count(bits >= cand) >= cap, cand, thr)

    thr = lax.fori_loop(0, 31, search, jnp.zeros((ne, 1, 1), jnp.int32))
    gt = bits > thr
    eq = bits == thr
    need = cap - count(gt)

    li = lax.broadcasted_iota(jnp.int32, (LANES, LANES), 0)
    lj = lax.broadcasted_iota(jnp.int32, (LANES, LANES), 1)
    before_lane = (li < lj).astype(BF16)
    upto_lane = (li <= lj).astype(BF16)
    all_lanes = jnp.ones((LANES, LANES), BF16)
    ri = lax.broadcasted_iota(jnp.int32, (nr, nr), 0)
    rj = lax.broadcasted_iota(jnp.int32, (nr, nr), 1)
    before_row = (rj < ri).astype(BF16)

    def prefix(m, lane_mat):
        m2 = m.astype(BF16).reshape(ne * nr, LANES)
        within = _dot(m2, lane_mat).reshape(ne, nr, LANES)
        tot = _dot(m2, all_lanes).reshape(ne, nr, LANES)
        roff = jnp.stack([_dot(before_row, tot[e].astype(BF16)) for e in range(ne)])
        return within, tot, roff

    w_eq, _, r_eq = prefix(eq, before_lane)
    sel = gt | (eq & (w_eq + r_eq < need))
    w_sel, t_sel, r_sel = prefix(sel, upto_lane)
    off_ref[0] = r_sel.astype(jnp.int32)
    w_ref[...] = w_sel
    tot_ref[...] = t_sel
    roff_ref[...] = r_sel

    slot = lax.broadcasted_iota(jnp.int32, (nr, capp), 1).astype(F32)
    rowid = lax.broadcasted_iota(jnp.int32, (nr, capp), 0).astype(F32)
    laneid = lax.broadcasted_iota(jnp.int32, (LANES, capp), 0).astype(F32)
    tn = (((0,), (0,)), ((), ()))

    def per_expert(e, carry):
        roff = roff_ref[e][:, 0:1]
        rincl = roff + tot_ref[e][:, 0:1]
        row_of = jnp.sum((rincl <= slot).astype(F32), axis=0, keepdims=True)
        onehot = rowid == row_of
        base = jnp.sum(jnp.where(onehot, roff, 0.0), axis=0, keepdims=True)
        onehot = onehot.astype(BF16)
        counts = lax.dot_general(w_ref[e].astype(BF16), onehot, tn, preferred_element_type=F32)
        col_of = jnp.sum((counts <= slot[0:1] - base).astype(F32), axis=0, keepdims=True)
        hi, mid, lo = _split3(aff_ref[0, e])
        vals = (lax.dot_general(hi, onehot, tn, preferred_element_type=F32)
                + lax.dot_general(mid, onehot, tn, preferred_element_type=F32)
                + lax.dot_general(lo, onehot, tn, preferred_element_type=F32))
        val = jnp.sum(jnp.where(laneid == col_of, vals, 0.0), axis=0, keepdims=True)
        live = slot[0:1] < cap
        idx_ref[0, pl.ds(e, 1), :] = jnp.where(live, row_of * LANES + col_of, 0.0).astype(jnp.int32)
        val_ref[0, pl.ds(e, 1), :] = jnp.where(live, val, 0.0)
        return carry

    lax.fori_loop(0, ne, per_expert, 0)


def _route(aff4, cap):
    b, ne, nr, _ = aff4.shape
    capp = -(-cap // LANES) * LANES
    blk = pl.BlockSpec((1, ne, nr, LANES), lambda bi: (bi, 0, 0, 0))
    lst = pl.BlockSpec((1, ne, capp), lambda bi: (bi, 0, 0))
    return pl.pallas_call(
        functools.partial(_route_kernel, cap), name="route",
        grid=(b,),
        in_specs=[blk],
        out_specs=[lst, lst, blk],
        out_shape=[jax.ShapeDtypeStruct((b, ne, capp), jnp.int32), jax.ShapeDtypeStruct((b, ne, capp), F32),
                   jax.ShapeDtypeStruct((b, ne, nr, LANES), jnp.int32)],
        scratch_shapes=[pltpu.VMEM((ne, nr, LANES), F32)] * 3,
        compiler_params=_cparams(("arbitrary",)),
    )(aff4)


MOE_SLOTS = 320
MOE_RANGE = 2048
ROW_UNROLL = 8


def _moe_kernel(nrange, nchunk, lo_ref, idx_ref, val_ref, h_ref, wg_ref, wu_ref, wd_ref, o_ref, xg_ref, yb_ref):
    bi, r, e = pl.program_id(0), pl.program_id(1), pl.program_id(2)
    rng = h_ref.shape[1] // nchunk

    @pl.when(e == 0)
    def _():
        o_ref[...] = jnp.zeros_like(o_ref)

    base = (bi * pl.num_programs(2) + e) * (nrange + 1) + r
    lo, hi = lo_ref[base], lo_ref[base + 1]
    tok0 = r * rng

    def tile_of(row):
        return pl.ds(pl.multiple_of(row * nchunk, nchunk), nchunk)

    def chunk(ci, carry):
        j0 = lo + ci * MOE_SLOTS
        cnt = jnp.minimum(MOE_SLOTS, hi - j0)
        xg_ref[...] = jnp.zeros_like(xg_ref)

        def gather(jj, c):
            xg_ref[tile_of(jj), :] = h_ref[0, tile_of(idx_ref[0, 0, j0 + jj] - tok0), :]
            return c

        def gather_group(gi, c):
            j = gi * ROW_UNROLL
            rows = [h_ref[0, tile_of(idx_ref[0, 0, j0 + j + u] - tok0), :] for u in range(ROW_UNROLL)]
            for u in range(ROW_UNROLL):
                xg_ref[tile_of(j + u), :] = rows[u]
            return c

        ngroup = cnt // ROW_UNROLL
        lax.fori_loop(0, ngroup, gather_group, 0)
        lax.fori_loop(ngroup * ROW_UNROLL, cnt, gather, 0)
        xb = _load_row_tiles(xg_ref, (), MOE_SLOTS).astype(BF16)
        hid = _silu(_dot(xb, wg_ref[0])) * _dot(xb, wu_ref[0])
        _store_row_tiles(yb_ref, (), _dot(hid.astype(BF16), wd_ref[0]))

        def scatter(jj, c):
            dst = tile_of(idx_ref[0, 0, j0 + jj] - tok0)
            o_ref[0, dst, :] += val_ref[0, 0, j0 + jj] * yb_ref[tile_of(jj), :]
            return c

        def scatter_group(gi, c):
            j = gi * ROW_UNROLL
            dsts = [tile_of(idx_ref[0, 0, j0 + j + u] - tok0) for u in range(ROW_UNROLL)]
            rows = [o_ref[0, dsts[u], :] + val_ref[0, 0, j0 + j + u] * yb_ref[tile_of(j + u), :]
                    for u in range(ROW_UNROLL)]
            for u in range(ROW_UNROLL):
                o_ref[0, dsts[u], :] = rows[u]
            return c

        lax.fori_loop(0, ngroup, scatter_group, 0)
        lax.fori_loop(ngroup * ROW_UNROLL, cnt, scatter, 0)
        return carry

    lax.fori_loop(0, (hi - lo + MOE_SLOTS - 1) // MOE_SLOTS, chunk, 0)


def _moe(h2, idx, vals, off, cap, wg, wu, wd, rng):
    ne, d, f = wg.shape
    nchunk = d // LANES
    b, t = h2.shape[0], h2.shape[1] // nchunk
    nrange = t // rng
    capp = idx.shape[-1]
    lo = off[:, :, ::rng // LANES, 0][:, :, :nrange]
    lo = jnp.concatenate([lo, jnp.full((b, ne, 1), cap, jnp.int32)], axis=-1).reshape(-1)
    lst = pl.BlockSpec((1, 1, capp), lambda bi, r, e, lo_ref: (bi * ne + e, 0, 0), memory_space=pltpu.SMEM)
    tile = pl.BlockSpec((1, rng * nchunk, LANES), lambda bi, r, e, lo_ref: (bi, r, 0))
    return pl.pallas_call(
        functools.partial(_moe_kernel, nrange, nchunk), name="moe",
        grid_spec=pltpu.PrefetchScalarGridSpec(
            num_scalar_prefetch=1,
            grid=(b, nrange, ne),
            in_specs=[lst, lst, tile,
                      pl.BlockSpec((1, d, f), lambda bi, r, e, lo_ref: (e, 0, 0)),
                      pl.BlockSpec((1, d, f), lambda bi, r, e, lo_ref: (e, 0, 0)),
                      pl.BlockSpec((1, f, d), lambda bi, r, e, lo_ref: (e, 0, 0))],
            out_specs=tile,
            scratch_shapes=[pltpu.VMEM((MOE_SLOTS * nchunk, LANES), F32)] * 2),
        out_shape=jax.ShapeDtypeStruct(h2.shape, F32),
        compiler_params=_cparams(("arbitrary", "arbitrary", "arbitrary")),
    )(lo, idx.reshape(b * ne, 1, capp), vals.reshape(b * ne, 1, capp), h2, wg, wu, wd)


def _residual_kernel(x_ref, y_ref, g_ref, o_ref):
    o_ref[0] = x_ref[0] + g_ref[0] * _load_row_tiles(y_ref, (0,), x_ref.shape[1])


def _residual(x, y, gate, tm):
    b, t, d = x.shape
    row = pl.BlockSpec((1, tm, d), lambda bi, i: (bi, i, 0))
    return pl.pallas_call(
        _residual_kernel, name="residual", grid=(b, t // tm),
        in_specs=[row, pl.BlockSpec((1, tm * d // LANES, LANES), lambda bi, i: (bi, i, 0)),
                  pl.BlockSpec((1, 1, d), lambda bi, i: (bi, 0, 0))],
        out_specs=row, out_shape=jax.ShapeDtypeStruct((b, t, d), F32),
        compiler_params=_cparams(("arbitrary", "arbitrary")),
    )(x, y, gate)


def _hgrn_lower_bounds(logits):
    p = jax.nn.softmax(logits.astype(F32), axis=1)
    return jnp.concatenate([jnp.zeros_like(p[:, :1]), jnp.cumsum(p[:, 1:], axis=1)], axis=1)


def _expert_choice_ffn(h2, aff, wg, wu, wd, rng):
    b, ne, t = aff.shape
    cap = EC_CAPACITY_FACTOR * t // ne
    rows = max(t // LANES, 8)
    pad = rows * LANES - t
    if pad:
        aff = jnp.pad(aff, ((0, 0), (0, 0), (0, pad)), constant_values=-1.0)
    idx, vals, off = _route(aff.reshape(b, ne, rows, LANES), cap)
    return _moe(h2, idx, vals, off, cap, wg, wu, wd, rng)


def kernel(x, c, ctx, c_ctx, w_mod, b_mod, norm1_w, w_in, na_q_norm, na_k_norm, na_rpb, hg_lb_logits, hg_norm,
           conv_w, w_out, norm2_w, w_router, w_exp_gate, w_exp_up, w_exp_down):
    b, t, d = x.shape
    wg, wu, wd = w_exp_gate.astype(BF16), w_exp_up.astype(BF16), w_exp_down.astype(BF16)
    depth = w_mod.shape[0]
    ctx_len = ctx.shape[1]
    cond = jnp.zeros((8, d), F32).at[0:b].set(c).at[b].set(c_ctx)
    mod = _modvec(cond, w_mod, b_mod).reshape(depth, 8, N_MOD, d)
    cos, sa, sb = _rope_tables(t)
    cos_c = jnp.ones((ctx_len, LANES), F32)
    zero_c = jnp.zeros((ctx_len, LANES), F32)
    lb = _hgrn_lower_bounds(hg_lb_logits)
    w_in_p = _permute_w_in(w_in)
    w_out_b = w_out.astype(BF16)
    w_router_t = w_router.transpose(0, 2, 1).astype(BF16)
    bias = jax.vmap(lambda r: _na_bias_tiles(r, t // GRID_W))(na_rpb)
    tile2 = lambda v: jnp.tile(v, LANES // HEAD_DIM)[None]

    cx = ctx
    y = y_c = None
    for l in range(depth):
        need_ctx = l < depth - 1
        m_lat = [mod[l, 0:b, j][:, None, :] for j in range(N_MOD)]
        m_ctx = [jnp.broadcast_to(mod[l, b, j][None, None, :], (b, 1, d)) for j in range(N_MOD)]
        g_prev = [mod[l - 1, 0:b, 5][:, None, :], jnp.broadcast_to(mod[l - 1, b, 5][None, None, :], (b, 1, d))] if l else [None, None]
        qn, kn = tile2(na_q_norm[l]), tile2(na_k_norm[l])
        x, qkv, rest = _inproj(x, y, g_prev[0], m_lat[0], m_lat[1], norm1_w[l][None], w_in_p[l], cos, sa, sb, qn, kn, 512)
        cx, qkv_c, rest_c = _inproj(cx, y_c, g_prev[1], m_ctx[0], m_ctx[1], norm1_w[l][None], w_in_p[l],
                                    cos_c, zero_c, zero_c, qn, kn, ctx_len)
        na = _na_attention(qkv, qkv_c, bias[l])
        o = o_c = None
        for rev in (False, True):
            lbd = lb[1 if rev else 0, l][None]
            o, o_c = _gla_scan(rest, rest_c, jnp.log(lbd), jnp.log1p(-lbd), rev, o, o_c,
                               jnp.tile(hg_norm[l], HG_HEADS)[None] if rev else None)
        x, h2, aff = _outproj(na, o, rest, x, w_out_b[l], m_lat[2], m_lat[3], m_lat[4], norm2_w[l][None],
                              conv_w[l], w_router_t[l], 512)
        y = _expert_choice_ffn(h2, aff, wg[l], wu[l], wd[l], min(MOE_RANGE, t))
        if need_ctx:
            na_c = _ctx_attention(qkv_c)
            cx, h2_c, aff_c = _outproj(na_c, o_c, rest_c, cx, w_out_b[l], m_ctx[2], m_ctx[3], m_ctx[4],
                                       norm2_w[l][None], conv_w[l], w_router_t[l], ctx_len)
            y_c = _expert_choice_ffn(h2_c, aff_c, wg[l], wu[l], wd[l], ctx_len)
    return _residual(x, y, mod[depth - 1, 0:b, 5][:, None, :], 512)
```

```python
import functools

import jax
import jax.numpy as jnp
import numpy as np
from jax import lax
from jax.experimental import pallas as pl
from jax.experimental.pallas import tpu as pltpu

GRID_W = 64
HEAD_DIM = 64
NA_HEADS = 6
NA_DIM = NA_HEADS * HEAD_DIM
NA_ROWS = 8
NA_COLS = 16
HG_HEADS = 6
HG_DIM = HG_HEADS * HEAD_DIM
HG_CHUNK = 64
HG_SUB = 8
CONV_DIM = 256
N_EXPERTS = 16
EC_CAPACITY_FACTOR = 2
ROPE_BASE = 10000.0
NORM_EPS = 1e-6
N_MOD = 6

LANES = 128
NA_QROWS = 4
NA_KROWS = NA_QROWS + NA_ROWS - 1
VMEM_LIMIT = 56 * 1024 * 1024

F32 = jnp.float32
BF16 = jnp.bfloat16
NEG_INF = float("-inf")


def _cparams(sem):
    return pltpu.CompilerParams(dimension_semantics=sem, vmem_limit_bytes=VMEM_LIMIT)


def _dot(a, b):
    return jnp.dot(a, b, preferred_element_type=F32)


def _dot_nt(a, b):
    return lax.dot_general(a, b, (((1,), (1,)), ((), ())), preferred_element_type=F32)


def _split3(x):
    hi = x.astype(BF16)
    r1 = x - hi.astype(F32)
    mid = r1.astype(BF16)
    lo = (r1 - mid.astype(F32)).astype(BF16)
    return hi, mid, lo


def _dot_sel_rhs(sel, x):
    hi, mid, lo = _split3(x)
    return _dot(sel, hi) + _dot(sel, mid) + _dot(sel, lo)


def _dot_sel_lhs(x, sel):
    hi, mid, lo = _split3(x)
    return _dot(hi, sel) + _dot(mid, sel) + _dot(lo, sel)


def _head_ones():
    r = lax.broadcasted_iota(jnp.int32, (LANES, LANES), 0) // HEAD_DIM
    c = lax.broadcasted_iota(jnp.int32, (LANES, LANES), 1) // HEAD_DIM
    return (r == c).astype(BF16)


def _silu(x):
    return x * (1.0 / (1.0 + jnp.exp(-x)))


def _load_row_tiles(ref, lead, nrows):
    nchunk = ref.shape[-2] // nrows
    return jnp.concatenate([ref[lead + (pl.ds(c, nrows, stride=nchunk), slice(None))] for c in range(nchunk)],
                           axis=1)


def _store_row_tiles(ref, lead, x):
    nrows = x.shape[0]
    nchunk = x.shape[1] // LANES
    for c in range(nchunk):
        ref[lead + (pl.ds(c, nrows, stride=nchunk), slice(None))] = x[:, c * LANES:(c + 1) * LANES]


def _modvec_kernel(cond_ref, w_ref, b_ref, o_ref):
    a = _silu(cond_ref[...]).astype(BF16)
    o_ref[0] = _dot(a, w_ref[0].astype(BF16)) + b_ref[0]


def _modvec(cond, w_mod, b_mod):
    depth, d, n = w_mod.shape
    tn = 512
    return pl.pallas_call(
        _modvec_kernel, name="modvec",
        grid=(depth, n // tn),
        in_specs=[
            pl.BlockSpec((8, d), lambda l, j: (0, 0)),
            pl.BlockSpec((1, d, tn), lambda l, j: (l, 0, j)),
            pl.BlockSpec((1, 1, tn), lambda l, j: (l, 0, j)),
        ],
        out_specs=pl.BlockSpec((1, 8, tn), lambda l, j: (l, 0, j)),
        out_shape=jax.ShapeDtypeStruct((depth, 8, n), F32),
        compiler_params=_cparams(("arbitrary", "arbitrary")),
    )(cond, w_mod, b_mod.reshape(depth, 1, n))


QKV_COLS = 3 * NA_DIM
REST_COLS = 3 * CONV_DIM + 5 * HG_DIM


def _rms_rows(x, w):
    ms = jnp.mean(x * x, axis=-1, keepdims=True)
    return x * lax.rsqrt(ms + NORM_EPS) * w


def _inproj_kernel(has_y, *refs):
    if has_y:
        (x_ref, y_ref, g_ref, sh_ref, sc_ref, nw_ref, w_ref, cos_ref, sa_ref, sb_ref, qn_ref, kn_ref,
         xo_ref, qkv_ref, rest_ref) = refs
        x = x_ref[0] + g_ref[0] * _load_row_tiles(y_ref, (0,), x_ref.shape[1])
        xo_ref[0] = x
    else:
        (x_ref, sh_ref, sc_ref, nw_ref, w_ref, cos_ref, sa_ref, sb_ref, qn_ref, kn_ref,
         qkv_ref, rest_ref) = refs
        x = x_ref[0]
    h = _rms_rows(x, nw_ref[...]) * (1.0 + sc_ref[0]) + sh_ref[0]
    hb = h.astype(BF16)
    ones = _head_ones()
    cos = cos_ref[...]
    sa = sa_ref[...]
    sb = sb_ref[...]
    pq = _dot(hb, w_ref[:, 0:QKV_COLS])
    for c in range(2 * NA_DIM // LANES):
        p = pq[:, c * LANES:(c + 1) * LANES]
        ssq = _dot_sel_lhs(p * p, ones)
        nw = qn_ref[...] if c < NA_DIM // LANES else kn_ref[...]
        pn = p * lax.rsqrt(ssq * (1.0 / HEAD_DIM) + NORM_EPS) * nw
        pr = pn * cos + pltpu.roll(pn, LANES - 16, 1) * sa + pltpu.roll(pn, 16, 1) * sb
        if c < NA_DIM // LANES:
            pr = pr * (HEAD_DIM ** -0.5)
        qkv_ref[0, :, c * LANES:(c + 1) * LANES] = pr.astype(BF16)
    qkv_ref[0, :, 2 * NA_DIM:] = pq[:, 2 * NA_DIM:].astype(BF16)
    rest_ref[0] = _dot(hb, w_ref[:, QKV_COLS:])


def _inproj(x, y, gate, shift, scale, norm_w, w_in, cos, sa, sb, qn, kn, tm):
    b, t, d = x.shape
    has_y = y is not None
    row = pl.BlockSpec((1, tm, d), lambda bi, i: (bi, i, 0))
    vec = pl.BlockSpec((1, 1, d), lambda bi, i: (bi, 0, 0))
    tab = pl.BlockSpec((tm, LANES), lambda bi, i: (i, 0))
    small = pl.BlockSpec((1, LANES), lambda bi, i: (0, 0))
    row_tiles = pl.BlockSpec((1, tm * d // LANES, LANES), lambda bi, i: (bi, i, 0))
    in_specs = [row] + ([row_tiles, vec] if has_y else []) + [
        vec, vec, pl.BlockSpec((1, d), lambda bi, i: (0, 0)),
        pl.BlockSpec(w_in.shape, lambda bi, i: (0, 0)), tab, tab, tab, small, small]
    out_specs = ([row] if has_y else []) + [
        pl.BlockSpec((1, tm, QKV_COLS), lambda bi, i: (bi, i, 0)),
        pl.BlockSpec((1, tm, REST_COLS), lambda bi, i: (bi, i, 0))]
    out_shape = ([jax.ShapeDtypeStruct((b, t, d), F32)] if has_y else []) + [
        jax.ShapeDtypeStruct((b, t, QKV_COLS), BF16), jax.ShapeDtypeStruct((b, t, REST_COLS), F32)]
    args = [x] + ([y, gate] if has_y else []) + [shift, scale, norm_w, w_in, cos, sa, sb, qn, kn]
    outs = pl.pallas_call(
        functools.partial(_inproj_kernel, has_y), name="inproj",
        grid=(b, t // tm), in_specs=in_specs, out_specs=out_specs, out_shape=out_shape,
        compiler_params=_cparams(("arbitrary", "arbitrary")),
    )(*args)
    return outs if has_y else [x] + list(outs)


def _rope_tables(n):
    t = jnp.arange(n, dtype=jnp.int32)
    row = (t // GRID_W).astype(F32)
    col = (t % GRID_W).astype(F32)
    n_freq = HEAD_DIM // 4
    inv = ROPE_BASE ** (-jnp.arange(n_freq, dtype=F32) / n_freq)
    ar = row[:, None] * inv
    ac = col[:, None] * inv
    ang = jnp.concatenate([ar, ar, ac, ac], axis=-1)
    cos = jnp.tile(jnp.cos(ang), (1, LANES // HEAD_DIM))
    sin = jnp.tile(jnp.sin(ang), (1, LANES // HEAD_DIM))
    quarter = (np.arange(LANES) // 16) % 2
    sa = sin * jnp.asarray(np.where(quarter == 0, -1.0, 0.0), F32)
    sb = sin * jnp.asarray(np.where(quarter == 1, 1.0, 0.0), F32)
    return cos, sa, sb


def _permute_w_in(w):
    qkv = w[..., :QKV_COLS]
    hg = w[..., QKV_COLS:QKV_COLS + 5 * HG_DIM]
    cv = w[..., QKV_COLS + 5 * HG_DIM:]
    return jnp.concatenate([qkv, cv, hg], axis=-1).astype(BF16)


def _na_bias_tiles(rpb, rows):
    nh, n_rel = rpb.shape[0], 2 * NA_ROWS - 1
    qc = np.arange(GRID_W)[:, None]
    kc = np.arange(GRID_W)[None, :]
    ws = np.clip(qc - NA_COLS // 2, 0, GRID_W - NA_COLS)
    col_ok = (kc >= ws) & (kc < ws + NA_COLS)
    rel_col = np.clip(kc - qc + NA_COLS - 1, 0, 2 * NA_COLS - 2)
    blocks = jnp.where(jnp.asarray(col_ok), rpb[:, :, rel_col], NEG_INF)
    blocks = jnp.concatenate([blocks, jnp.full((nh, 1, GRID_W, GRID_W), NEG_INF, rpb.dtype)], axis=1)
    qi = np.arange(NA_QROWS)[:, None]
    kj = np.arange(NA_KROWS)[None, :]
    which = []
    for r0 in (0, NA_QROWS, rows - NA_QROWS):
        kb = int(np.clip(r0 - NA_ROWS // 2, 0, rows - NA_KROWS))
        qr, kr = r0 + qi, kb + kj
        rs = np.clip(qr - NA_ROWS // 2, 0, rows - NA_ROWS)
        ok = (kr >= rs) & (kr < rs + NA_ROWS)
        which.append(np.where(ok, kr - qr + NA_ROWS - 1, n_rel))
    which = np.stack(which).astype(np.int32)
    tiles = blocks[:, which]
    return tiles.transpose(1, 0, 2, 4, 3, 5).reshape(3, nh, NA_QROWS * GRID_W, NA_KROWS * GRID_W)


def _softmax_pv(scores, values):
    m = scores[0].max(axis=-1, keepdims=True)
    for s in scores[1:]:
        m = jnp.maximum(m, s.max(axis=-1, keepdims=True))
    acc, den = None, None
    for s, v in zip(scores, values):
        p = jnp.exp(s - m)
        d = p.sum(axis=-1, keepdims=True)
        o = _dot(p.astype(BF16), v)
        acc = o if acc is None else acc + o
        den = d if den is None else den + d
    return acc / den


def _na_kernel(rows, q_ref, k_ref, v_ref, kc_ref, vc_ref, bias_ref, o_ref):
    i = pl.program_id(2)
    kb = jnp.clip(i * NA_QROWS - NA_ROWS // 2, 0, rows - NA_KROWS)
    start = pl.multiple_of(kb * GRID_W, GRID_W)
    q = q_ref[0]
    kt = k_ref[0, pl.ds(start, NA_KROWS * GRID_W), :]
    vt = v_ref[0, pl.ds(start, NA_KROWS * GRID_W), :]
    kc = kc_ref[0]
    vc = vc_ref[0]
    lane = lax.broadcasted_iota(jnp.int32, q.shape, 1)
    outs = []
    for hh in range(LANES // HEAD_DIM):
        qh = jnp.where((lane // HEAD_DIM) == hh, q, jnp.zeros_like(q))
        s_loc = _dot_nt(qh, kt) + bias_ref[0, hh]
        s_ctx = _dot_nt(qh, kc)
        outs.append(_softmax_pv([s_loc, s_ctx], [vt, vc]))
    o_ref[0] = jnp.where(lane < HEAD_DIM, outs[0], outs[1])


def _na_attention(qkv, qkv_c, bias):
    b, t, _ = qkv.shape
    ctx_len = qkv_c.shape[1]
    rows = t // GRID_W
    nq, nk = NA_QROWS * GRID_W, NA_KROWS * GRID_W
    nt = t // nq
    nhp = NA_DIM // LANES

    def variant(i):
        return jnp.where(i == 0, 0, jnp.where(i == nt - 1, 2, 1))

    return pl.pallas_call(
        functools.partial(_na_kernel, rows), name="na_attn",
        grid=(b, nhp, nt),
        in_specs=[
            pl.BlockSpec((1, nq, LANES), lambda bi, hp, i: (bi, i, hp)),
            pl.BlockSpec((1, t, LANES), lambda bi, hp, i: (bi, 0, nhp + hp)),
            pl.BlockSpec((1, t, LANES), lambda bi, hp, i: (bi, 0, 2 * nhp + hp)),
            pl.BlockSpec((1, ctx_len, LANES), lambda bi, hp, i: (bi, 0, nhp + hp)),
            pl.BlockSpec((1, ctx_len, LANES), lambda bi, hp, i: (bi, 0, 2 * nhp + hp)),
            pl.BlockSpec((1, LANES // HEAD_DIM, nq, nk), lambda bi, hp, i: (variant(i), hp, 0, 0)),
        ],
        out_specs=pl.BlockSpec((1, nq, LANES), lambda bi, hp, i: (bi, i, hp)),
        out_shape=jax.ShapeDtypeStruct((b, t, NA_DIM), F32),
        compiler_params=_cparams(("arbitrary", "arbitrary", "arbitrary")),
    )(qkv, qkv, qkv, qkv_c, qkv_c, bias)


def _ctx_attn_kernel(q_ref, k_ref, v_ref, o_ref):
    q = q_ref[0]
    lane = lax.broadcasted_iota(jnp.int32, q.shape, 1)
    outs = []
    for hh in range(LANES // HEAD_DIM):
        qh = jnp.where((lane // HEAD_DIM) == hh, q, jnp.zeros_like(q))
        outs.append(_softmax_pv([_dot_nt(qh, k_ref[0])], [v_ref[0]]))
    o_ref[0] = jnp.where(lane < HEAD_DIM, outs[0], outs[1])


def _ctx_attention(qkv_c):
    b, n, _ = qkv_c.shape
    nhp = NA_DIM // LANES
    return pl.pallas_call(
        _ctx_attn_kernel, name="ctx_attn",
        grid=(b, nhp),
        in_specs=[pl.BlockSpec((1, n, LANES), lambda bi, hp: (bi, 0, hp)),
                  pl.BlockSpec((1, n, LANES), lambda bi, hp: (bi, 0, nhp + hp)),
                  pl.BlockSpec((1, n, LANES), lambda bi, hp: (bi, 0, 2 * nhp + hp))],
        out_specs=pl.BlockSpec((1, n, LANES), lambda bi, hp: (bi, 0, hp)),
        out_shape=jax.ShapeDtypeStruct((b, n, NA_DIM), F32),
        compiler_params=_cparams(("arbitrary", "arbitrary")),
    )(qkv_c, qkv_c, qkv_c)


REST_HG_BLOCK = {"q": 2, "f_fwd": 3, "f_bwd": 4, "i": 5, "g": 6}


LOG2E = 1.4426950408889634
GLA_GROUP = 2


def _gla_gates(z, lbl, l1m):
    soft = jnp.log1p(jnp.exp(-jnp.abs(z)))
    y = l1m + (jnp.minimum(z, 0.0) - soft)
    hi = jnp.maximum(lbl, y)
    lo = jnp.minimum(lbl, y)
    log_f = hi + jnp.log1p(jnp.exp(lo - hi))
    log_k = l1m + (jnp.minimum(-z, 0.0) - soft)
    return log_f * LOG2E, log_k * LOG2E


def _gla_chunk_stages(q, lk, g, v, rev):
    n, sub, tile = HG_CHUNK, HG_SUB, 8
    width = q.shape[1]
    nheads = width // HEAD_DIM
    pairs = [slice(p * LANES, (p + 1) * LANES) for p in range(width // LANES)]
    head_of_lane = lax.broadcasted_iota(jnp.int32, (sub, width), 1) // HEAD_DIM
    trow = lax.broadcasted_iota(jnp.int32, (tile, width), 0)
    ti = lax.broadcasted_iota(jnp.int32, (n, n), 0)
    si = lax.broadcasted_iota(jnp.int32, (n, n), 1)
    tri = ((ti <= si) if rev else (ti >= si)).astype(BF16)
    a = _dot_sel_rhs(tri, g)
    b = a - lk
    vb = v.astype(BF16)

    qe = (q * jnp.exp2(a)).astype(BF16)
    a_end = a[0:1] if rev else a[n - 1:n]
    kd_end = jnp.exp2(a_end - b).astype(BF16)
    decay = jnp.exp2(a_end)
    bi = lax.broadcasted_iota(jnp.int32, (LANES, LANES), 0) // HEAD_DIM
    bj = lax.broadcasted_iota(jnp.int32, (LANES, LANES), 1) // HEAD_DIM
    tn = (((0,), (0,)), ((), ()))
    grow = [jnp.where(bi == bj, lax.dot_general(vb[:, p], kd_end[:, p], tn, preferred_element_type=F32), 0.0)
            for p in pairs]

    def state(st):
        o_state = jnp.concatenate([_dot_nt(qe[:, p], st[j].astype(BF16)) for j, p in enumerate(pairs)], axis=1)
        return o_state, [st[j] * decay[:, p] + grow[j] for j, p in enumerate(pairs)]

    ones = _head_ones()

    def far_scores(i):
        lo = i * sub
        other = slice(lo + sub, n) if rev else slice(0, lo)
        if other.stop == other.start:
            return None
        r = a[lo + sub - 1:lo + sub] if rev else a[lo:lo + 1]
        qd = q[lo:lo + sub] * jnp.exp2(a[lo:lo + sub] - r)
        kd = jnp.exp2(r - b[other]).astype(BF16)
        stacked = jnp.concatenate([jnp.where(head_of_lane == h, qd, 0.0) for h in range(nheads)], axis=0)
        return other, _dot_nt(stacked.astype(BF16), kd)

    def far_values(far):
        if far is None:
            return None
        other, scores = far
        return _dot(scores.astype(BF16), vb[other])

    def far_pick(o_all):
        pick = o_all[0:sub]
        for h in range(1, nheads):
            pick = jnp.where(head_of_lane == h, o_all[h * sub:(h + 1) * sub], pick)
        return pick

    def near_scores(i):
        lo = i * sub
        prods, pieces = [], []
        for sl in range(sub):
            s = lo + sl
            for h in range(sub // tile):
                t0 = h * tile
                full = (t0 + tile - 1 <= sl) if rev else (t0 >= sl)
                skip = (t0 > sl) if rev else (t0 + tile - 1 < sl)
                if skip:
                    continue
                rows = slice(lo + t0, lo + t0 + tile)
                d = a[rows] - b[s:s + 1]
                if not full:
                    keep = (trow + t0 <= sl) if rev else (trow + t0 >= sl)
                    d = jnp.where(keep, d, NEG_INF)
                prods.append(q[rows] * jnp.exp2(d))
                pieces.append((h, sl))
        pr = jnp.concatenate(prods, axis=0).astype(BF16)
        return pieces, jnp.concatenate([_dot(pr[:, p], ones) for p in pairs], axis=1)

    def near_values(i, pieces, sc):
        lo = i * sub
        acc = [None] * (sub // tile)
        for j, (h, sl) in enumerate(pieces):
            term = sc[j * tile:(j + 1) * tile] * v[lo + sl:lo + sl + 1]
            acc[h] = term if acc[h] is None else acc[h] + term
        return jnp.concatenate(acc, axis=0)

    return state, far_scores, far_values, far_pick, near_scores, near_values


def _gla_chunks(chunks, st, rev):
    nblk = HG_CHUNK // HG_SUB
    stages = [_gla_chunk_stages(*c, rev) for c in chunks]
    o_state = []
    for state, *_ in stages:
        o, st = state(st)
        o_state.append(o)
    far = [[s[1](i) for i in range(nblk)] for s in stages]
    near = [[None] * nblk for _ in stages]
    far_o = [[None] * nblk for _ in stages]
    for i in range(nblk):
        for c, s in enumerate(stages):
            near[c][i] = s[4](i)
            far_o[c][i] = s[2](far[c][i])
    outs = []
    for c, s in enumerate(stages):
        parts = []
        for i in range(nblk):
            o_i = s[5](i, *near[c][i])
            parts.append(o_i if far_o[c][i] is None else o_i + s[3](far_o[c][i]))
        outs.append(jnp.concatenate(parts, axis=0) + o_state[c])
    return outs, st


def _gla_kernel(rev, final, *refs):
    if final:
        (q_ref, f_ref, v_ref, qc_ref, fc_ref, vc_ref, lbl_ref, l1m_ref,
         op_ref, g_ref, opc_ref, gc_ref, nw_ref, o_ref, oc_ref, st_ref) = refs
    else:
        (q_ref, f_ref, v_ref, qc_ref, fc_ref, vc_ref, lbl_ref, l1m_ref,
         o_ref, oc_ref, st_ref) = refs
    step = pl.program_id(1)

    def scan_block(qr, fr, vr, outr, prevr, gater):
        nchunk = qr.shape[1] // HG_CHUNK

        def body(j, carry):
            npair = HG_DIM // LANES
            rows, chunks = [], []
            for u in range(GLA_GROUP):
                c = (nchunk - 1 - (j * GLA_GROUP + u)) if rev else j * GLA_GROUP + u
                r = pl.ds(pl.multiple_of(c * HG_CHUNK, HG_CHUNK), HG_CHUNK)
                log_f, log_k = _gla_gates(fr[0, r, :], lbl_ref[...], l1m_ref[...])
                chunks.append((qr[0, r, :] * (HEAD_DIM ** -0.5), log_k, log_f, vr[0, r, :]))
                rows.append(r)
            outs, st_new = _gla_chunks(chunks, [st_ref[p] for p in range(npair)], rev)
            for p in range(npair):
                st_ref[p] = st_new[p]
            for r, o in zip(rows, outs):
                if final:
                    tot = o + prevr[0, r, :]
                    sq = tot * tot
                    ms = jnp.concatenate([_dot_sel_lhs(sq[:, p * LANES:(p + 1) * LANES], _head_ones())
                                          for p in range(npair)], axis=1) * (1.0 / HEAD_DIM)
                    o = tot * lax.rsqrt(ms + NORM_EPS) * nw_ref[...] * _silu(gater[0, r, :])
                outr[0, r, :] = o
            return carry

        lax.fori_loop(0, nchunk // GLA_GROUP, body, 0)

    @pl.when(step == 0)
    def _():
        st_ref[...] = jnp.zeros_like(st_ref)
        scan_block(qc_ref, fc_ref, vc_ref, oc_ref, opc_ref if final else None, gc_ref if final else None)

    @pl.when(step > 0)
    def _():
        scan_block(q_ref, f_ref, v_ref, o_ref, op_ref if final else None, g_ref if final else None)


def _gla_scan(rest, rest_c, lbl, l1m, rev, prev=None, prev_c=None, norm_w=None, tb=512):
    b, t, _ = rest.shape
    ctx_len = rest_c.shape[1]
    nblk = t // tb
    final = prev is not None
    fkey = "f_bwd" if rev else "f_fwd"

    def blk(s):
        return (nblk - jnp.maximum(s, 1)) if rev else jnp.maximum(s - 1, 0)

    def lat(key):
        off = REST_HG_BLOCK[key]
        return pl.BlockSpec((1, tb, HG_DIM), lambda bi, s: (bi, blk(s), off))

    def ctx(key):
        off = REST_HG_BLOCK[key]
        return pl.BlockSpec((1, ctx_len, HG_DIM), lambda bi, s: (bi, 0, off))

    vec = pl.BlockSpec((1, HG_DIM), lambda bi, s: (0, 0))
    o_lat = pl.BlockSpec((1, tb, HG_DIM), lambda bi, s: (bi, blk(s), 0))
    o_ctx = pl.BlockSpec((1, ctx_len, HG_DIM), lambda bi, s: (bi, 0, 0))
    in_specs = [lat("q"), lat(fkey), lat("i"), ctx("q"), ctx(fkey), ctx("i"), vec, vec]
    args = [rest, rest, rest, rest_c, rest_c, rest_c, lbl, l1m]
    if final:
        in_specs += [o_lat, lat("g"), o_ctx, ctx("g"), vec]
        args += [prev, rest, prev_c, rest_c, norm_w]
    return pl.pallas_call(
        functools.partial(_gla_kernel, rev, final), name="gla_bwd" if rev else "gla_fwd",
        grid=(b, nblk + 1),
        in_specs=in_specs,
        out_specs=[o_lat, o_ctx],
        out_shape=[jax.ShapeDtypeStruct((b, t, HG_DIM), F32), jax.ShapeDtypeStruct((b, ctx_len, HG_DIM), F32)],
        scratch_shapes=[pltpu.VMEM((HG_DIM // LANES, LANES, LANES), F32)],
        compiler_params=_cparams(("arbitrary", "arbitrary")),
    )(*args)


HALO = 8


def _outproj_kernel(na_ref, hg_ref, cb_ref, cc_ref, cu_ref, ccp_ref, cup_ref, ccn_ref, cun_ref, x_ref,
                    w_ref, g_ref, sh_ref, sc_ref, nw_ref, cw_ref, wr_ref, xo_ref, h_ref, aff_ref):
    i = pl.program_id(1)
    last = pl.num_programs(1) - 1
    z = cc_ref[0] * cu_ref[0]
    tm = z.shape[0]
    row = lax.broadcasted_iota(jnp.int32, z.shape, 0)
    z_before = jnp.where(i > 0, ccp_ref[0, HALO - 1:HALO, :] * cup_ref[0, HALO - 1:HALO, :], 0.0)
    z_after = jnp.where(i < last, ccn_ref[0, 0:1, :] * cun_ref[0, 0:1, :], 0.0)
    zp = jnp.where(row == 0, z_before, pltpu.roll(z, 1, 0))
    zn = jnp.where(row == tm - 1, z_after, pltpu.roll(z, tm - 1, 0))
    cv = cb_ref[0] * (cw_ref[0:1, :] * zp + cw_ref[1:2, :] * z + cw_ref[2:3, :] * zn)
    mix = jnp.concatenate([na_ref[0], hg_ref[0], cv], axis=-1).astype(BF16)
    x = x_ref[0] + g_ref[0] * _dot(mix, w_ref[...])
    xo_ref[0] = x
    h = _rms_rows(x, nw_ref[...]) * (1.0 + sc_ref[0]) + sh_ref[0]
    _store_row_tiles(h_ref, (0,), h)
    logits = _dot_nt(wr_ref[...], h.astype(BF16))
    p = jnp.exp(logits - logits.max(axis=0, keepdims=True))
    aff_ref[0] = p / p.sum(axis=0, keepdims=True)


def _outproj(na, hg, rest, x, w_out, gate, shift, scale, norm_w, conv_w, w_router_t, tm):
    b, t, d = x.shape
    nh = tm // HALO
    row = pl.BlockSpec((1, tm, d), lambda bi, i: (bi, i, 0))
    vec = pl.BlockSpec((1, 1, d), lambda bi, i: (bi, 0, 0))
    mixer = pl.BlockSpec((1, tm, NA_DIM), lambda bi, i: (bi, i, 0))

    def conv(c):
        return pl.BlockSpec((1, tm, CONV_DIM), lambda bi, i: (bi, i, c))

    def before(c):
        return pl.BlockSpec((1, HALO, CONV_DIM), lambda bi, i: (bi, jnp.maximum(i * nh - 1, 0), c))

    def after(c):
        return pl.BlockSpec((1, HALO, CONV_DIM), lambda bi, i: (bi, jnp.minimum((i + 1) * nh, t // HALO - 1), c))

    return pl.pallas_call(
        _outproj_kernel, name="outproj",
        grid=(b, t // tm),
        in_specs=[mixer, mixer, conv(0), conv(1), conv(2), before(1), before(2), after(1), after(2), row,
                  pl.BlockSpec(w_out.shape, lambda bi, i: (0, 0)), vec, vec, vec,
                  pl.BlockSpec((1, d), lambda bi, i: (0, 0)),
                  pl.BlockSpec(conv_w.shape, lambda bi, i: (0, 0)),
                  pl.BlockSpec(w_router_t.shape, lambda bi, i: (0, 0))],
        out_specs=[row, pl.BlockSpec((1, tm * d // LANES, LANES), lambda bi, i: (bi, i, 0)),
                   pl.BlockSpec((1, N_EXPERTS, tm), lambda bi, i: (bi, 0, i))],
        out_shape=[jax.ShapeDtypeStruct((b, t, d), F32), jax.ShapeDtypeStruct((b, t * d // LANES, LANES), F32),
                   jax.ShapeDtypeStruct((b, N_EXPERTS, t), F32)],
        compiler_params=_cparams(("arbitrary", "arbitrary")),
    )(na, hg, rest, rest, rest, rest, rest, rest, rest, x, w_out, gate, shift, scale, norm_w, conv_w, w_router_t)


def _route_kernel(cap, aff_ref, idx_ref, val_ref, off_ref, w_ref, tot_ref, roff_ref):
    ne, nr, _ = aff_ref.shape[1:]
    capp = idx_ref.shape[2]
    a = aff_ref[0]
    bits = pltpu.bitcast(a, jnp.int32)

    def count(m):
        return jnp.sum(jnp.sum(m.astype(F32), axis=2, keepdims=True), axis=1, keepdims=True)

    def search(k, thr):
        cand = thr | jnp.left_shift(jnp.int32(1), 30 - k)
        return jnp.where(count(bits >= cand) >= cap, cand, thr)

    thr = lax.fori_loop(0, 31, search, jnp.zeros((ne, 1, 1), jnp.int32))
    gt = bits > thr
    eq = bits == thr
    need = cap - count(gt)

    li = lax.broadcasted_iota(jnp.int32, (LANES, LANES), 0)
    lj = lax.broadcasted_iota(jnp.int32, (LANES, LANES), 1)
    before_lane = (li < lj).astype(BF16)
    upto_lane = (li <= lj).astype(BF16)
    all_lanes = jnp.ones((LANES, LANES), BF16)
    ri = lax.broadcasted_iota(jnp.int32, (nr, nr), 0)
    rj = lax.broadcasted_iota(jnp.int32, (nr, nr), 1)
    before_row = (rj < ri).astype(BF16)

    def prefix(m, lane_mat):
        m2 = m.astype(BF16).reshape(ne * nr, LANES)
        within = _dot(m2, lane_mat).reshape(ne, nr, LANES)
        tot = _dot(m2, all_lanes).reshape(ne, nr, LANES)
        roff = jnp.stack([_dot(before_row, tot[e].astype(BF16)) for e in range(ne)])
        return within, tot, roff

    w_eq, _, r_eq = prefix(eq, before_lane)
    sel = gt | (eq & (w_eq + r_eq < need))
    w_sel, t_sel, r_sel = prefix(sel, upto_lane)
    off_ref[0] = r_sel.astype(jnp.int32)
    w_ref[...] = w_sel
    tot_ref[...] = t_sel
    roff_ref[...] = r_sel

    slot = lax.broadcasted_iota(jnp.int32, (nr, capp), 1).astype(F32)
    rowid = lax.broadcasted_iota(jnp.int32, (nr, capp), 0).astype(F32)
    laneid = lax.broadcasted_iota(jnp.int32, (LANES, capp), 0).astype(F32)
    tn = (((0,), (0,)), ((), ()))

    def per_expert(e, carry):
        roff = roff_ref[e][:, 0:1]
        rincl = roff + tot_ref[e][:, 0:1]
        row_of = jnp.sum((rincl <= slot).astype(F32), axis=0, keepdims=True)
        onehot = rowid == row_of
        base = jnp.sum(jnp.where(onehot, roff, 0.0), axis=0, keepdims=True)
        onehot = onehot.astype(BF16)
        counts = lax.dot_general(w_ref[e].astype(BF16), onehot, tn, preferred_element_type=F32)
        col_of = jnp.sum((counts <= slot[0:1] - base).astype(F32), axis=0, keepdims=True)
        hi, mid, lo = _split3(aff_ref[0, e])
        vals = (lax.dot_general(hi, onehot, tn, preferred_element_type=F32)
                + lax.dot_general(mid, onehot, tn, preferred_element_type=F32)
                + lax.dot_general(lo, onehot, tn, preferred_element_type=F32))
        val = jnp.sum(jnp.where(laneid == col_of, vals, 0.0), axis=0, keepdims=True)
        live = slot[0:1] < cap
        idx_ref[0, pl.ds(e, 1), :] = jnp.where(live, row_of * LANES + col_of, 0.0).astype(jnp.int32)
        val_ref[0, pl.ds(e, 1), :] = jnp.where(live, val, 0.0)
        return carry

    lax.fori_loop(0, ne, per_expert, 0)


def _route(aff4, cap):
    b, ne, nr, _ = aff4.shape
    capp = -(-cap // LANES) * LANES
    blk = pl.BlockSpec((1, ne, nr, LANES), lambda bi: (bi, 0, 0, 0))
    lst = pl.BlockSpec((1, ne, capp), lambda bi: (bi, 0, 0))
    return pl.pallas_call(
        functools.partial(_route_kernel, cap), name="route",
        grid=(b,),
        in_specs=[blk],
        out_specs=[lst, lst, blk],
        out_shape=[jax.ShapeDtypeStruct((b, ne, capp), jnp.int32), jax.ShapeDtypeStruct((b, ne, capp), F32),
                   jax.ShapeDtypeStruct((b, ne, nr, LANES), jnp.int32)],
        scratch_shapes=[pltpu.VMEM((ne, nr, LANES), F32)] * 3,
        compiler_params=_cparams(("arbitrary",)),
    )(aff4)


MOE_SLOTS = 320
MOE_RANGE = 2048
ROW_UNROLL = 8


def _moe_kernel(nrange, nchunk, lo_ref, idx_ref, val_ref, h_ref, wg_ref, wu_ref, wd_ref, o_ref, xg_ref, yb_ref):
    bi, r, e = pl.program_id(0), pl.program_id(1), pl.program_id(2)
    rng = h_ref.shape[1] // nchunk

    @pl.when(e == 0)
    def _():
        o_ref[...] = jnp.zeros_like(o_ref)

    base = (bi * pl.num_programs(2) + e) * (nrange + 1) + r
    lo, hi = lo_ref[base], lo_ref[base + 1]
    tok0 = r * rng

    def tile_of(row):
        return pl.ds(pl.multiple_of(row * nchunk, nchunk), nchunk)

    def chunk(ci, carry):
        j0 = lo + ci * MOE_SLOTS
        cnt = jnp.minimum(MOE_SLOTS, hi - j0)
        xg_ref[...] = jnp.zeros_like(xg_ref)

        def gather(jj, c):
            xg_ref[tile_of(jj), :] = h_ref[0, tile_of(idx_ref[0, 0, j0 + jj] - tok0), :]
            return c

        def gather_group(gi, c):
            j = gi * ROW_UNROLL
            rows = [h_ref[0, tile_of(idx_ref[0, 0, j0 + j + u] - tok0), :] for u in range(ROW_UNROLL)]
            for u in range(ROW_UNROLL):
                xg_ref[tile_of(j + u), :] = rows[u]
            return c

        ngroup = cnt // ROW_UNROLL
        lax.fori_loop(0, ngroup, gather_group, 0)
        lax.fori_loop(ngroup * ROW_UNROLL, cnt, gather, 0)
        xb = _load_row_tiles(xg_ref, (), MOE_SLOTS).astype(BF16)
        hid = _silu(_dot(xb, wg_ref[0])) * _dot(xb, wu_ref[0])
        _store_row_tiles(yb_ref, (), _dot(hid.astype(BF16), wd_ref[0]))

        def scatter(jj, c):
            dst = tile_of(idx_ref[0, 0, j0 + jj] - tok0)
            o_ref[0, dst, :] += val_ref[0, 0, j0 + jj] * yb_ref[tile_of(jj), :]
            return c

        def scatter_group(gi, c):
            j = gi * ROW_UNROLL
            dsts = [tile_of(idx_ref[0, 0, j0 + j + u] - tok0) for u in range(ROW_UNROLL)]
            rows = [o_ref[0, dsts[u], :] + val_ref[0, 0, j0 + j + u] * yb_ref[tile_of(j + u), :]
                    for u in range(ROW_UNROLL)]
            for u in range(ROW_UNROLL):
                o_ref[0, dsts[u], :] = rows[u]
            return c

        lax.fori_loop(0, ngroup, scatter_group, 0)
        lax.fori_loop(ngroup * ROW_UNROLL, cnt, scatter, 0)
        return carry

    lax.fori_loop(0, (hi - lo + MOE_SLOTS - 1) // MOE_SLOTS, chunk, 0)


def _moe(h2, idx, vals, off, cap, wg, wu, wd, rng):
    ne, d, f = wg.shape
    nchunk = d // LANES
    b, t = h2.shape[0], h2.shape[1] // nchunk
    nrange = t // rng
    capp = idx.shape[-1]
    lo = off[:, :, ::rng // LANES, 0][:, :, :nrange]
    lo = jnp.concatenate([lo, jnp.full((b, ne, 1), cap, jnp.int32)], axis=-1).reshape(-1)
    lst = pl.BlockSpec((1, 1, capp), lambda bi, r, e, lo_ref: (bi * ne + e, 0, 0), memory_space=pltpu.SMEM)
    tile = pl.BlockSpec((1, rng * nchunk, LANES), lambda bi, r, e, lo_ref: (bi, r, 0))
    return pl.pallas_call(
        functools.partial(_moe_kernel, nrange, nchunk), name="moe",
        grid_spec=pltpu.PrefetchScalarGridSpec(
            num_scalar_prefetch=1,
            grid=(b, nrange, ne),
            in_specs=[lst, lst, tile,
                      pl.BlockSpec((1, d, f), lambda bi, r, e, lo_ref: (e, 0, 0)),
                      pl.BlockSpec((1, d, f), lambda bi, r, e, lo_ref: (e, 0, 0)),
                      pl.BlockSpec((1, f, d), lambda bi, r, e, lo_ref: (e, 0, 0))],
            out_specs=tile,
            scratch_shapes=[pltpu.VMEM((MOE_SLOTS * nchunk, LANES), F32)] * 2),
        out_shape=jax.ShapeDtypeStruct(h2.shape, F32),
        compiler_params=_cparams(("arbitrary", "arbitrary", "arbitrary")),
    )(lo, idx.reshape(b * ne, 1, capp), vals.reshape(b * ne, 1, capp), h2, wg, wu, wd)


def _residual_kernel(x_ref, y_ref, g_ref, o_ref):
    o_ref[0] = x_ref[0] + g_ref[0] * _load_row_tiles(y_ref, (0,), x_ref.shape[1])


def _residual(x, y, gate, tm):
    b, t, d = x.shape
    row = pl.BlockSpec((1, tm, d), lambda bi, i: (bi, i, 0))
    return pl.pallas_call(
        _residual_kernel, name="residual", grid=(b, t // tm),
        in_specs=[row, pl.BlockSpec((1, tm * d // LANES, LANES), lambda bi, i: (bi, i, 0)),
                  pl.BlockSpec((1, 1, d), lambda bi, i: (bi, 0, 0))],
        out_specs=row, out_shape=jax.ShapeDtypeStruct((b, t, d), F32),
        compiler_params=_cparams(("arbitrary", "arbitrary")),
    )(x, y, gate)


def _hgrn_lower_bounds(logits):
    p = jax.nn.softmax(logits.astype(F32), axis=1)
    return jnp.concatenate([jnp.zeros_like(p[:, :1]), jnp.cumsum(p[:, 1:], axis=1)], axis=1)


def _expert_choice_ffn(h2, aff, wg, wu, wd, rng):
    b, ne, t = aff.shape
    cap = EC_CAPACITY_FACTOR * t // ne
    rows = max(t // LANES, 8)
    pad = rows * LANES - t
    if pad:
        aff = jnp.pad(aff, ((0, 0), (0, 0), (0, pad)), constant_values=-1.0)
    idx, vals, off = _route(aff.reshape(b, ne, rows, LANES), cap)
    return _moe(h2, idx, vals, off, cap, wg, wu, wd, rng)


def kernel(x, c, ctx, c_ctx, w_mod, b_mod, norm1_w, w_in, na_q_norm, na_k_norm, na_rpb, hg_lb_logits, hg_norm,
           conv_w, w_out, norm2_w, w_router, w_exp_gate, w_exp_up, w_exp_down):
    b, t, d = x.shape
    wg, wu, wd = w_exp_gate.astype(BF16), w_exp_up.astype(BF16), w_exp_down.astype(BF16)
    depth = w_mod.shape[0]
    ctx_len = ctx.shape[1]
    cond = jnp.zeros((8, d), F32).at[0:b].set(c).at[b].set(c_ctx)
    mod = _modvec(cond, w_mod, b_mod).reshape(depth, 8, N_MOD, d)
    cos, sa, sb = _rope_tables(t)
    cos_c = jnp.ones((ctx_len, LANES), F32)
    zero_c = jnp.zeros((ctx_len, LANES), F32)
    lb = _hgrn_lower_bounds(hg_lb_logits)
    w_in_p = _permute_w_in(w_in)
    w_out_b = w_out.astype(BF16)
    w_router_t = w_router.transpose(0, 2, 1).astype(BF16)
    bias = jax.vmap(lambda r: _na_bias_tiles(r, t // GRID_W))(na_rpb)
    tile2 = lambda v: jnp.tile(v, LANES // HEAD_DIM)[None]

    cx = ctx
    y = y_c = None
    for l in range(depth):
        need_ctx = l < depth - 1
        m_lat = [mod[l, 0:b, j][:, None, :] for j in range(N_MOD)]
        m_ctx = [jnp.broadcast_to(mod[l, b, j][None, None, :], (b, 1, d)) for j in range(N_MOD)]
        g_prev = [mod[l - 1, 0:b, 5][:, None, :], jnp.broadcast_to(mod[l - 1, b, 5][None, None, :], (b, 1, d))] if l else [None, None]
        qn, kn = tile2(na_q_norm[l]), tile2(na_k_norm[l])
        x, qkv, rest = _inproj(x, y, g_prev[0], m_lat[0], m_lat[1], norm1_w[l][None], w_in_p[l], cos, sa, sb, qn, kn, 512)
        cx, qkv_c, rest_c = _inproj(cx, y_c, g_prev[1], m_ctx[0], m_ctx[1], norm1_w[l][None], w_in_p[l],
                                    cos_c, zero_c, zero_c, qn, kn, ctx_len)
        na = _na_attention(qkv, qkv_c, bias[l])
        o = o_c = None
        for rev in (False, True):
            lbd = lb[1 if rev else 0, l][None]
            o, o_c = _gla_scan(rest, rest_c, jnp.log(lbd), jnp.log1p(-lbd), rev, o, o_c,
                               jnp.tile(hg_norm[l], HG_HEADS)[None] if rev else None)
        x, h2, aff = _outproj(na, o, rest, x, w_out_b[l], m_lat[2], m_lat[3], m_lat[4], norm2_w[l][None],
                              conv_w[l], w_router_t[l], 512)
        y = _expert_choice_ffn(h2, aff, wg[l], wu[l], wd[l], min(MOE_RANGE, t))
        if need_ctx:
            na_c = _ctx_attention(qkv_c)
            cx, h2_c, aff_c = _outproj(na_c, o_c, rest_c, cx, w_out_b[l], m_ctx[2], m_ctx[3], m_ctx[4],
                                       norm2_w[l][None], conv_w[l], w_router_t[l], ctx_len)
            y_c = _expert_choice_ffn(h2_c, aff_c, wg[l], wu[l], wd[l], ctx_len)
    return _residual(x, y, mod[depth - 1, 0:b, 5][:, None, :], 512)
```

```python
import functools

import jax
import jax.numpy as jnp
import numpy as np
from jax import lax
from jax.experimental import pallas as pl
from jax.experimental.pallas import tpu as pltpu

GRID_W = 64
HEAD_DIM = 64
NA_HEADS = 6
NA_DIM = NA_HEADS * HEAD_DIM
NA_ROWS = 8
NA_COLS = 16
HG_HEADS = 6
HG_DIM = HG_HEADS * HEAD_DIM
HG_CHUNK = 64
HG_SUB = 8
CONV_DIM = 256
N_EXPERTS = 16
EC_CAPACITY_FACTOR = 2
ROPE_BASE = 10000.0
NORM_EPS = 1e-6
N_MOD = 6

LANES = 128
NA_QROWS = 4
NA_KROWS = NA_QROWS + NA_ROWS - 1
VMEM_LIMIT = 56 * 1024 * 1024

F32 = jnp.float32
BF16 = jnp.bfloat16
NEG_INF = float("-inf")


def _cparams(sem):
    return pltpu.CompilerParams(dimension_semantics=sem, vmem_limit_bytes=VMEM_LIMIT)


def _dot(a, b):
    return jnp.dot(a, b, preferred_element_type=F32)


def _dot_nt(a, b):
    return lax.dot_general(a, b, (((1,), (1,)), ((), ())), preferred_element_type=F32)


def _split3(x):
    hi = x.astype(BF16)
    r1 = x - hi.astype(F32)
    mid = r1.astype(BF16)
    lo = (r1 - mid.astype(F32)).astype(BF16)
    return hi, mid, lo


def _dot_sel_rhs(sel, x):
    hi, mid, lo = _split3(x)
    return _dot(sel, hi) + _dot(sel, mid) + _dot(sel, lo)


def _dot_sel_lhs(x, sel):
    hi, mid, lo = _split3(x)
    return _dot(hi, sel) + _dot(mid, sel) + _dot(lo, sel)


def _head_ones():
    r = lax.broadcasted_iota(jnp.int32, (LANES, LANES), 0) // HEAD_DIM
    c = lax.broadcasted_iota(jnp.int32, (LANES, LANES), 1) // HEAD_DIM
    return (r == c).astype(BF16)


def _silu(x):
    return x * (1.0 / (1.0 + jnp.exp(-x)))


def _load_row_tiles(ref, lead, nrows):
    nchunk = ref.shape[-2] // nrows
    return jnp.concatenate([ref[lead + (pl.ds(c, nrows, stride=nchunk), slice(None))] for c in range(nchunk)],
                           axis=1)


def _store_row_tiles(ref, lead, x):
    nrows = x.shape[0]
    nchunk = x.shape[1] // LANES
    for c in range(nchunk):
        ref[lead + (pl.ds(c, nrows, stride=nchunk), slice(None))] = x[:, c * LANES:(c + 1) * LANES]


def _modvec_kernel(cond_ref, w_ref, b_ref, o_ref):
    a = _silu(cond_ref[...]).astype(BF16)
    o_ref[0] = _dot(a, w_ref[0].astype(BF16)) + b_ref[0]


def _modvec(cond, w_mod, b_mod):
    depth, d, n = w_mod.shape
    tn = 512
    return pl.pallas_call(
        _modvec_kernel, name="modvec",
        grid=(depth, n // tn),
        in_specs=[
            pl.BlockSpec((8, d), lambda l, j: (0, 0)),
            pl.BlockSpec((1, d, tn), lambda l, j: (l, 0, j)),
            pl.BlockSpec((1, 1, tn), lambda l, j: (l, 0, j)),
        ],
        out_specs=pl.BlockSpec((1, 8, tn), lambda l, j: (l, 0, j)),
        out_shape=jax.ShapeDtypeStruct((depth, 8, n), F32),
        compiler_params=_cparams(("arbitrary", "arbitrary")),
    )(cond, w_mod, b_mod.reshape(depth, 1, n))


QKV_COLS = 3 * NA_DIM
REST_COLS = 3 * CONV_DIM + 5 * HG_DIM


def _rms_rows(x, w):
    ms = jnp.mean(x * x, axis=-1, keepdims=True)
    return x * lax.rsqrt(ms + NORM_EPS) * w


def _inproj_kernel(has_y, *refs):
    if has_y:
        (x_ref, y_ref, g_ref, sh_ref, sc_ref, nw_ref, w_ref, cos_ref, sa_ref, sb_ref, qn_ref, kn_ref,
         xo_ref, qkv_ref, rest_ref) = refs
        x = x_ref[0] + g_ref[0] * _load_row_tiles(y_ref, (0,), x_ref.shape[1])
        xo_ref[0] = x
    else:
        (x_ref, sh_ref, sc_ref, nw_ref, w_ref, cos_ref, sa_ref, sb_ref, qn_ref, kn_ref,
         qkv_ref, rest_ref) = refs
        x = x_ref[0]
    h = _rms_rows(x, nw_ref[...]) * (1.0 + sc_ref[0]) + sh_ref[0]
    hb = h.astype(BF16)
    ones = _head_ones()
    cos = cos_ref[...]
    sa = sa_ref[...]
    sb = sb_ref[...]
    pq = _dot(hb, w_ref[:, 0:QKV_COLS])
    for c in range(2 * NA_DIM // LANES):
        p = pq[:, c * LANES:(c + 1) * LANES]
        ssq = _dot_sel_lhs(p * p, ones)
        nw = qn_ref[...] if c < NA_DIM // LANES else kn_ref[...]
        pn = p * lax.rsqrt(ssq * (1.0 / HEAD_DIM) + NORM_EPS) * nw
        pr = pn * cos + pltpu.roll(pn, LANES - 16, 1) * sa + pltpu.roll(pn, 16, 1) * sb
        if c < NA_DIM // LANES:
            pr = pr * (HEAD_DIM ** -0.5)
        qkv_ref[0, :, c * LANES:(c + 1) * LANES] = pr.astype(BF16)
    qkv_ref[0, :, 2 * NA_DIM:] = pq[:, 2 * NA_DIM:].astype(BF16)
    rest_ref[0] = _dot(hb, w_ref[:, QKV_COLS:])


def _inproj(x, y, gate, shift, scale, norm_w, w_in, layer, cos, sa, sb, qn, kn, tm):
    b, t, d = x.shape
    has_y = y is not None
    row = pl.BlockSpec((1, tm, d), lambda bi, i: (bi, i, 0))
    vec = pl.BlockSpec((1, 1, d), lambda bi, i: (bi, 0, 0))
    tab = pl.BlockSpec((tm, LANES), lambda bi, i: (i, 0))
    small = pl.BlockSpec((1, LANES), lambda bi, i: (0, 0))
    row_tiles = pl.BlockSpec((1, tm * d // LANES, LANES), lambda bi, i: (bi, i, 0))
    in_specs = [row] + ([row_tiles, vec] if has_y else []) + [
        vec, vec, pl.BlockSpec((1, d), lambda bi, i: (0, 0)),
        pl.BlockSpec((None,) + w_in.shape[1:], lambda bi, i: (layer, 0, 0)), tab, tab, tab, small, small]
    out_specs = ([row] if has_y else []) + [
        pl.BlockSpec((1, tm, QKV_COLS), lambda bi, i: (bi, i, 0)),
        pl.BlockSpec((1, tm, REST_COLS), lambda bi, i: (bi, i, 0))]
    out_shape = ([jax.ShapeDtypeStruct((b, t, d), F32)] if has_y else []) + [
        jax.ShapeDtypeStruct((b, t, QKV_COLS), BF16), jax.ShapeDtypeStruct((b, t, REST_COLS), F32)]
    args = [x] + ([y, gate] if has_y else []) + [shift, scale, norm_w, w_in, cos, sa, sb, qn, kn]
    outs = pl.pallas_call(
        functools.partial(_inproj_kernel, has_y), name="inproj",
        grid=(b, t // tm), in_specs=in_specs, out_specs=out_specs, out_shape=out_shape,
        compiler_params=_cparams(("arbitrary", "arbitrary")),
    )(*args)
    return outs if has_y else [x] + list(outs)


def _rope_tables(n):
    t = jnp.arange(n, dtype=jnp.int32)
    row = (t // GRID_W).astype(F32)
    col = (t % GRID_W).astype(F32)
    n_freq = HEAD_DIM // 4
    inv = ROPE_BASE ** (-jnp.arange(n_freq, dtype=F32) / n_freq)
    ar = row[:, None] * inv
    ac = col[:, None] * inv
    ang = jnp.concatenate([ar, ar, ac, ac], axis=-1)
    cos = jnp.tile(jnp.cos(ang), (1, LANES // HEAD_DIM))
    sin = jnp.tile(jnp.sin(ang), (1, LANES // HEAD_DIM))
    quarter = (np.arange(LANES) // 16) % 2
    sa = sin * jnp.asarray(np.where(quarter == 0, -1.0, 0.0), F32)
    sb = sin * jnp.asarray(np.where(quarter == 1, 1.0, 0.0), F32)
    return cos, sa, sb


def _permute_w_in(w):
    qkv = w[..., :QKV_COLS]
    hg = w[..., QKV_COLS:QKV_COLS + 5 * HG_DIM]
    cv = w[..., QKV_COLS + 5 * HG_DIM:]
    return jnp.concatenate([qkv, cv, hg], axis=-1).astype(BF16)


def _na_bias_tiles(rpb, rows):
    nh, n_rel = rpb.shape[0], 2 * NA_ROWS - 1
    qc = np.arange(GRID_W)[:, None]
    kc = np.arange(GRID_W)[None, :]
    ws = np.clip(qc - NA_COLS // 2, 0, GRID_W - NA_COLS)
    col_ok = (kc >= ws) & (kc < ws + NA_COLS)
    rel_col = np.clip(kc - qc + NA_COLS - 1, 0, 2 * NA_COLS - 2)
    blocks = jnp.where(jnp.asarray(col_ok), rpb[:, :, rel_col], NEG_INF)
    blocks = jnp.concatenate([blocks, jnp.full((nh, 1, GRID_W, GRID_W), NEG_INF, rpb.dtype)], axis=1)
    qi = np.arange(NA_QROWS)[:, None]
    kj = np.arange(NA_KROWS)[None, :]
    which = []
    for r0 in (0, NA_QROWS, rows - NA_QROWS):
        kb = int(np.clip(r0 - NA_ROWS // 2, 0, rows - NA_KROWS))
        qr, kr = r0 + qi, kb + kj
        rs = np.clip(qr - NA_ROWS // 2, 0, rows - NA_ROWS)
        ok = (kr >= rs) & (kr < rs + NA_ROWS)
        which.append(np.where(ok, kr - qr + NA_ROWS - 1, n_rel))
    which = np.stack(which).astype(np.int32)
    tiles = blocks[:, which]
    return tiles.transpose(1, 0, 2, 4, 3, 5).reshape(3, nh, NA_QROWS * GRID_W, NA_KROWS * GRID_W)


def _softmax_pv(scores, values):
    m = scores[0].max(axis=-1, keepdims=True)
    for s in scores[1:]:
        m = jnp.maximum(m, s.max(axis=-1, keepdims=True))
    acc, den = None, None
    for s, v in zip(scores, values):
        p = jnp.exp(s - m)
        d = p.sum(axis=-1, keepdims=True)
        o = _dot(p.astype(BF16), v)
        acc = o if acc is None else acc + o
        den = d if den is None else den + d
    return acc / den


def _na_kernel(rows, q_ref, k_ref, v_ref, kc_ref, vc_ref, bias_ref, o_ref):
    i = pl.program_id(2)
    kb = jnp.clip(i * NA_QROWS - NA_ROWS // 2, 0, rows - NA_KROWS)
    start = pl.multiple_of(kb * GRID_W, GRID_W)
    q = q_ref[0]
    kt = k_ref[0, pl.ds(start, NA_KROWS * GRID_W), :]
    vt = v_ref[0, pl.ds(start, NA_KROWS * GRID_W), :]
    kc = kc_ref[0]
    vc = vc_ref[0]
    lane = lax.broadcasted_iota(jnp.int32, q.shape, 1)
    outs = []
    for hh in range(LANES // HEAD_DIM):
        qh = jnp.where((lane // HEAD_DIM) == hh, q, jnp.zeros_like(q))
        s_loc = _dot_nt(qh, kt) + bias_ref[0, hh]
        s_ctx = _dot_nt(qh, kc)
        outs.append(_softmax_pv([s_loc, s_ctx], [vt, vc]))
    o_ref[0] = jnp.where(lane < HEAD_DIM, outs[0], outs[1])


def _na_attention(qkv, qkv_c, bias, layer):
    b, t, _ = qkv.shape
    ctx_len = qkv_c.shape[1]
    rows = t // GRID_W
    nq, nk = NA_QROWS * GRID_W, NA_KROWS * GRID_W
    nt = t // nq
    nhp = NA_DIM // LANES

    def variant(i):
        return jnp.where(i == 0, 0, jnp.where(i == nt - 1, 2, 1))

    return pl.pallas_call(
        functools.partial(_na_kernel, rows), name="na_attn",
        grid=(b, nhp, nt),
        in_specs=[
            pl.BlockSpec((1, nq, LANES), lambda bi, hp, i: (bi, i, hp)),
            pl.BlockSpec((1, t, LANES), lambda bi, hp, i: (bi, 0, nhp + hp)),
            pl.BlockSpec((1, t, LANES), lambda bi, hp, i: (bi, 0, 2 * nhp + hp)),
            pl.BlockSpec((1, ctx_len, LANES), lambda bi, hp, i: (bi, 0, nhp + hp)),
            pl.BlockSpec((1, ctx_len, LANES), lambda bi, hp, i: (bi, 0, 2 * nhp + hp)),
            pl.BlockSpec((None, 1, LANES // HEAD_DIM, nq, nk), lambda bi, hp, i: (layer, variant(i), hp, 0, 0)),
        ],
        out_specs=pl.BlockSpec((1, nq, LANES), lambda bi, hp, i: (bi, i, hp)),
        out_shape=jax.ShapeDtypeStruct((b, t, NA_DIM), F32),
        compiler_params=_cparams(("arbitrary", "arbitrary", "arbitrary")),
    )(qkv, qkv, qkv, qkv_c, qkv_c, bias)


def _ctx_attn_kernel(q_ref, k_ref, v_ref, o_ref):
    q = q_ref[0]
    lane = lax.broadcasted_iota(jnp.int32, q.shape, 1)
    outs = []
    for hh in range(LANES // HEAD_DIM):
        qh = jnp.where((lane // HEAD_DIM) == hh, q, jnp.zeros_like(q))
        outs.append(_softmax_pv([_dot_nt(qh, k_ref[0])], [v_ref[0]]))
    o_ref[0] = jnp.where(lane < HEAD_DIM, outs[0], outs[1])


def _ctx_attention(qkv_c):
    b, n, _ = qkv_c.shape
    nhp = NA_DIM // LANES
    return pl.pallas_call(
        _ctx_attn_kernel, name="ctx_attn",
        grid=(b, nhp),
        in_specs=[pl.BlockSpec((1, n, LANES), lambda bi, hp: (bi, 0, hp)),
                  pl.BlockSpec((1, n, LANES), lambda bi, hp: (bi, 0, nhp + hp)),
                  pl.BlockSpec((1, n, LANES), lambda bi, hp: (bi, 0, 2 * nhp + hp))],
        out_specs=pl.BlockSpec((1, n, LANES), lambda bi, hp: (bi, 0, hp)),
        out_shape=jax.ShapeDtypeStruct((b, n, NA_DIM), F32),
        compiler_params=_cparams(("arbitrary", "arbitrary")),
    )(qkv_c, qkv_c, qkv_c)


REST_HG_BLOCK = {"q": 2, "f_fwd": 3, "f_bwd": 4, "i": 5, "g": 6}


LOG2E = 1.4426950408889634
GLA_GROUP = 2


def _gla_gates(z, lbl, l1m):
    soft = jnp.log1p(jnp.exp(-jnp.abs(z)))
    y = l1m + (jnp.minimum(z, 0.0) - soft)
    hi = jnp.maximum(lbl, y)
    lo = jnp.minimum(lbl, y)
    log_f = hi + jnp.log1p(jnp.exp(lo - hi))
    log_k = l1m + (jnp.minimum(-z, 0.0) - soft)
    return log_f * LOG2E, log_k * LOG2E


def _gla_chunk_stages(q, lk, g, v, rev):
    n, sub, tile = HG_CHUNK, HG_SUB, 8
    width = q.shape[1]
    nheads = width // HEAD_DIM
    pairs = [slice(p * LANES, (p + 1) * LANES) for p in range(width // LANES)]
    head_of_lane = lax.broadcasted_iota(jnp.int32, (sub, width), 1) // HEAD_DIM
    trow = lax.broadcasted_iota(jnp.int32, (tile, width), 0)
    ti = lax.broadcasted_iota(jnp.int32, (n, n), 0)
    si = lax.broadcasted_iota(jnp.int32, (n, n), 1)
    tri = ((ti <= si) if rev else (ti >= si)).astype(BF16)
    a = _dot_sel_rhs(tri, g)
    b = a - lk
    vb = v.astype(BF16)

    qe = (q * jnp.exp2(a)).astype(BF16)
    a_end = a[0:1] if rev else a[n - 1:n]
    kd_end = jnp.exp2(a_end - b).astype(BF16)
    decay = jnp.exp2(a_end)
    bi = lax.broadcasted_iota(jnp.int32, (LANES, LANES), 0) // HEAD_DIM
    bj = lax.broadcasted_iota(jnp.int32, (LANES, LANES), 1) // HEAD_DIM
    tn = (((0,), (0,)), ((), ()))
    grow = [jnp.where(bi == bj, lax.dot_general(vb[:, p], kd_end[:, p], tn, preferred_element_type=F32), 0.0)
            for p in pairs]

    def state(st):
        o_state = jnp.concatenate([_dot_nt(qe[:, p], st[j].astype(BF16)) for j, p in enumerate(pairs)], axis=1)
        return o_state, [st[j] * decay[:, p] + grow[j] for j, p in enumerate(pairs)]

    ones = _head_ones()

    def far_scores(i):
        lo = i * sub
        other = slice(lo + sub, n) if rev else slice(0, lo)
        if other.stop == other.start:
            return None
        r = a[lo + sub - 1:lo + sub] if rev else a[lo:lo + 1]
        qd = q[lo:lo + sub] * jnp.exp2(a[lo:lo + sub] - r)
        kd = jnp.exp2(r - b[other]).astype(BF16)
        stacked = jnp.concatenate([jnp.where(head_of_lane == h, qd, 0.0) for h in range(nheads)], axis=0)
        return other, _dot_nt(stacked.astype(BF16), kd)

    def far_values(far):
        if far is None:
            return None
        other, scores = far
        return _dot(scores.astype(BF16), vb[other])

    def far_pick(o_all):
        pick = o_all[0:sub]
        for h in range(1, nheads):
            pick = jnp.where(head_of_lane == h, o_all[h * sub:(h + 1) * sub], pick)
        return pick

    def near_scores(i):
        lo = i * sub
        prods, pieces = [], []
        for sl in range(sub):
            s = lo + sl
            for h in range(sub // tile):
                t0 = h * tile
                full = (t0 + tile - 1 <= sl) if rev else (t0 >= sl)
                skip = (t0 > sl) if rev else (t0 + tile - 1 < sl)
                if skip:
                    continue
                rows = slice(lo + t0, lo + t0 + tile)
                d = a[rows] - b[s:s + 1]
                if not full:
                    keep = (trow + t0 <= sl) if rev else (trow + t0 >= sl)
                    d = jnp.where(keep, d, NEG_INF)
                prods.append(q[rows] * jnp.exp2(d))
                pieces.append((h, sl))
        pr = jnp.concatenate(prods, axis=0).astype(BF16)
        return pieces, jnp.concatenate([_dot(pr[:, p], ones) for p in pairs], axis=1)

    def near_values(i, pieces, sc):
        lo = i * sub
        acc = [None] * (sub // tile)
        for j, (h, sl) in enumerate(pieces):
            term = sc[j * tile:(j + 1) * tile] * v[lo + sl:lo + sl + 1]
            acc[h] = term if acc[h] is None else acc[h] + term
        return jnp.concatenate(acc, axis=0)

    return state, far_scores, far_values, far_pick, near_scores, near_values


def _gla_chunks(chunks, st, rev):
    nblk = HG_CHUNK // HG_SUB
    stages = [_gla_chunk_stages(*c, rev) for c in chunks]
    o_state = []
    for state, *_ in stages:
        o, st = state(st)
        o_state.append(o)
    far = [[s[1](i) for i in range(nblk)] for s in stages]
    near = [[None] * nblk for _ in stages]
    far_o = [[None] * nblk for _ in stages]
    for i in range(nblk):
        for c, s in enumerate(stages):
            near[c][i] = s[4](i)
            far_o[c][i] = s[2](far[c][i])
    outs = []
    for c, s in enumerate(stages):
        parts = []
        for i in range(nblk):
            o_i = s[5](i, *near[c][i])
            parts.append(o_i if far_o[c][i] is None else o_i + s[3](far_o[c][i]))
        outs.append(jnp.concatenate(parts, axis=0) + o_state[c])
    return outs, st


def _gla_kernel(rev, final, *refs):
    if final:
        (q_ref, f_ref, v_ref, qc_ref, fc_ref, vc_ref, lbl_ref, l1m_ref,
         op_ref, g_ref, opc_ref, gc_ref, nw_ref, o_ref, oc_ref, st_ref) = refs
    else:
        (q_ref, f_ref, v_ref, qc_ref, fc_ref, vc_ref, lbl_ref, l1m_ref,
         o_ref, oc_ref, st_ref) = refs
    step = pl.program_id(1)

    def scan_block(qr, fr, vr, outr, prevr, gater):
        nchunk = qr.shape[1] // HG_CHUNK

        def body(j, carry):
            npair = HG_DIM // LANES
            rows, chunks = [], []
            for u in range(GLA_GROUP):
                c = (nchunk - 1 - (j * GLA_GROUP + u)) if rev else j * GLA_GROUP + u
                r = pl.ds(pl.multiple_of(c * HG_CHUNK, HG_CHUNK), HG_CHUNK)
                log_f, log_k = _gla_gates(fr[0, r, :], lbl_ref[...], l1m_ref[...])
                chunks.append((qr[0, r, :] * (HEAD_DIM ** -0.5), log_k, log_f, vr[0, r, :]))
                rows.append(r)
            outs, st_new = _gla_chunks(chunks, [st_ref[p] for p in range(npair)], rev)
            for p in range(npair):
                st_ref[p] = st_new[p]
            for r, o in zip(rows, outs):
                if final:
                    tot = o + prevr[0, r, :]
                    sq = tot * tot
                    ms = jnp.concatenate([_dot_sel_lhs(sq[:, p * LANES:(p + 1) * LANES], _head_ones())
                                          for p in range(npair)], axis=1) * (1.0 / HEAD_DIM)
                    o = tot * lax.rsqrt(ms + NORM_EPS) * nw_ref[...] * _silu(gater[0, r, :])
                outr[0, r, :] = o
            return carry

        lax.fori_loop(0, nchunk // GLA_GROUP, body, 0)

    @pl.when(step == 0)
    def _():
        st_ref[...] = jnp.zeros_like(st_ref)
        scan_block(qc_ref, fc_ref, vc_ref, oc_ref, opc_ref if final else None, gc_ref if final else None)

    @pl.when(step > 0)
    def _():
        scan_block(q_ref, f_ref, v_ref, o_ref, op_ref if final else None, g_ref if final else None)


def _gla_scan(rest, rest_c, lbl, l1m, rev, prev=None, prev_c=None, norm_w=None, tb=512):
    b, t, _ = rest.shape
    ctx_len = rest_c.shape[1]
    nblk = t // tb
    final = prev is not None
    fkey = "f_bwd" if rev else "f_fwd"

    def blk(s):
        return (nblk - jnp.maximum(s, 1)) if rev else jnp.maximum(s - 1, 0)

    def lat(key):
        off = REST_HG_BLOCK[key]
        return pl.BlockSpec((1, tb, HG_DIM), lambda bi, s: (bi, blk(s), off))

    def ctx(key):
        off = REST_HG_BLOCK[key]
        return pl.BlockSpec((1, ctx_len, HG_DIM), lambda bi, s: (bi, 0, off))

    vec = pl.BlockSpec((1, HG_DIM), lambda bi, s: (0, 0))
    o_lat = pl.BlockSpec((1, tb, HG_DIM), lambda bi, s: (bi, blk(s), 0))
    o_ctx = pl.BlockSpec((1, ctx_len, HG_DIM), lambda bi, s: (bi, 0, 0))
    in_specs = [lat("q"), lat(fkey), lat("i"), ctx("q"), ctx(fkey), ctx("i"), vec, vec]
    args = [rest, rest, rest, rest_c, rest_c, rest_c, lbl, l1m]
    if final:
        in_specs += [o_lat, lat("g"), o_ctx, ctx("g"), vec]
        args += [prev, rest, prev_c, rest_c, norm_w]
    return pl.pallas_call(
        functools.partial(_gla_kernel, rev, final), name="gla_bwd" if rev else "gla_fwd",
        grid=(b, nblk + 1),
        in_specs=in_specs,
        out_specs=[o_lat, o_ctx],
        out_shape=[jax.ShapeDtypeStruct((b, t, HG_DIM), F32), jax.ShapeDtypeStruct((b, ctx_len, HG_DIM), F32)],
        scratch_shapes=[pltpu.VMEM((HG_DIM // LANES, LANES, LANES), F32)],
        compiler_params=_cparams(("arbitrary", "arbitrary")),
    )(*args)


HALO = 8


def _outproj_kernel(na_ref, hg_ref, cb_ref, cc_ref, cu_ref, ccp_ref, cup_ref, ccn_ref, cun_ref, x_ref,
                    w_ref, g_ref, sh_ref, sc_ref, nw_ref, cw_ref, wr_ref, xo_ref, h_ref, aff_ref):
    i = pl.program_id(1)
    last = pl.num_programs(1) - 1
    z = cc_ref[0] * cu_ref[0]
    tm = z.shape[0]
    row = lax.broadcasted_iota(jnp.int32, z.shape, 0)
    z_before = jnp.where(i > 0, ccp_ref[0, HALO - 1:HALO, :] * cup_ref[0, HALO - 1:HALO, :], 0.0)
    z_after = jnp.where(i < last, ccn_ref[0, 0:1, :] * cun_ref[0, 0:1, :], 0.0)
    zp = jnp.where(row == 0, z_before, pltpu.roll(z, 1, 0))
    zn = jnp.where(row == tm - 1, z_after, pltpu.roll(z, tm - 1, 0))
    cv = cb_ref[0] * (cw_ref[0:1, :] * zp + cw_ref[1:2, :] * z + cw_ref[2:3, :] * zn)
    mix = jnp.concatenate([na_ref[0], hg_ref[0], cv], axis=-1).astype(BF16)
    x = x_ref[0] + g_ref[0] * _dot(mix, w_ref[...])
    xo_ref[0] = x
    h = _rms_rows(x, nw_ref[...]) * (1.0 + sc_ref[0]) + sh_ref[0]
    _store_row_tiles(h_ref, (0,), h)
    logits = _dot_nt(wr_ref[...], h.astype(BF16))
    p = jnp.exp(logits - logits.max(axis=0, keepdims=True))
    aff_ref[0] = p / p.sum(axis=0, keepdims=True)


def _outproj(na, hg, rest, x, w_out, layer, gate, shift, scale, norm_w, conv_w, w_router_t, tm):
    b, t, d = x.shape
    nh = tm // HALO
    row = pl.BlockSpec((1, tm, d), lambda bi, i: (bi, i, 0))
    vec = pl.BlockSpec((1, 1, d), lambda bi, i: (bi, 0, 0))
    mixer = pl.BlockSpec((1, tm, NA_DIM), lambda bi, i: (bi, i, 0))

    def conv(c):
        return pl.BlockSpec((1, tm, CONV_DIM), lambda bi, i: (bi, i, c))

    def before(c):
        return pl.BlockSpec((1, HALO, CONV_DIM), lambda bi, i: (bi, jnp.maximum(i * nh - 1, 0), c))

    def after(c):
        return pl.BlockSpec((1, HALO, CONV_DIM), lambda bi, i: (bi, jnp.minimum((i + 1) * nh, t // HALO - 1), c))

    return pl.pallas_call(
        _outproj_kernel, name="outproj",
        grid=(b, t // tm),
        in_specs=[mixer, mixer, conv(0), conv(1), conv(2), before(1), before(2), after(1), after(2), row,
                  pl.BlockSpec((None,) + w_out.shape[1:], lambda bi, i: (layer, 0, 0)), vec, vec, vec,
                  pl.BlockSpec((1, d), lambda bi, i: (0, 0)),
                  pl.BlockSpec(conv_w.shape, lambda bi, i: (0, 0)),
                  pl.BlockSpec(w_router_t.shape, lambda bi, i: (0, 0))],
        out_specs=[row, pl.BlockSpec((1, tm * d // LANES, LANES), lambda bi, i: (bi, i, 0)),
                   pl.BlockSpec((1, N_EXPERTS, tm), lambda bi, i: (bi, 0, i))],
        out_shape=[jax.ShapeDtypeStruct((b, t, d), F32), jax.ShapeDtypeStruct((b, t * d // LANES, LANES), F32),
                   jax.ShapeDtypeStruct((b, N_EXPERTS, t), F32)],
        compiler_params=_cparams(("arbitrary", "arbitrary")),
    )(na, hg, rest, rest, rest, rest, rest, rest, rest, x, w_out, gate, shift, scale, norm_w, conv_w, w_router_t)


def _route_kernel(cap, aff_ref, idx_ref, val_ref, off_ref, w_ref, tot_ref, roff_ref):
    ne, nr, _ = aff_ref.shape[1:]
    capp = idx_ref.shape[2]
    a = aff_ref[0]
    bits = pltpu.bitcast(a, jnp.int32)

    def count(m):
        return jnp.sum(jnp.sum(m.astype(F32), axis=2, keepdims=True), axis=1, keepdims=True)

    def search(k, thr):
        cand = thr | jnp.left_shift(jnp.int32(1), 30 - k)
        return jnp.where(count(bits >= cand) >= cap, cand, thr)

    thr = lax.fori_loop(0, 31, search, jnp.zeros((ne, 1, 1), jnp.int32))
    gt = bits > thr
    eq = bits == thr
    need = cap - count(gt)

    li = lax.broadcasted_iota(jnp.int32, (LANES, LANES), 0)
    lj = lax.broadcasted_iota(jnp.int32, (LANES, LANES), 1)
    before_lane = (li < lj).astype(BF16)
    upto_lane = (li <= lj).astype(BF16)
    all_lanes = jnp.ones((LANES, LANES), BF16)
    ri = lax.broadcasted_iota(jnp.int32, (nr, nr), 0)
    rj = lax.broadcasted_iota(jnp.int32, (nr, nr), 1)
    before_row = (rj < ri).astype(BF16)

    def prefix(m, lane_mat):
        m2 = m.astype(BF16).reshape(ne * nr, LANES)
        within = _dot(m2, lane_mat).reshape(ne, nr, LANES)
        tot = _dot(m2, all_lanes).reshape(ne, nr, LANES)
        roff = jnp.stack([_dot(before_row, tot[e].astype(BF16)) for e in range(ne)])
        return within, tot, roff

    w_eq, _, r_eq = prefix(eq, before_lane)
    sel = gt | (eq & (w_eq + r_eq < need))
    w_sel, t_sel, r_sel = prefix(sel, upto_lane)
    off_ref[0] = r_sel.astype(jnp.int32)
    w_ref[...] = w_sel
    tot_ref[...] = t_sel
    roff_ref[...] = r_sel

    slot = lax.broadcasted_iota(jnp.int32, (nr, capp), 1).astype(F32)
    rowid = lax.broadcasted_iota(jnp.int32, (nr, capp), 0).astype(F32)
    laneid = lax.broadcasted_iota(jnp.int32, (LANES, capp), 0).astype(F32)
    tn = (((0,), (0,)), ((), ()))

    def per_expert(e, carry):
        roff = roff_ref[e][:, 0:1]
        rincl = roff + tot_ref[e][:, 0:1]
        row_of = jnp.sum((rincl <= slot).astype(F32), axis=0, keepdims=True)
        onehot = rowid == row_of
        base = jnp.sum(jnp.where(onehot, roff, 0.0), axis=0, keepdims=True)
        onehot = onehot.astype(BF16)
        counts = lax.dot_general(w_ref[e].astype(BF16), onehot, tn, preferred_element_type=F32)
        col_of = jnp.sum((counts <= slot[0:1] - base).astype(F32), axis=0, keepdims=True)
        hi, mid, lo = _split3(aff_ref[0, e])
        vals = (lax.dot_general(hi, onehot, tn, preferred_element_type=F32)
                + lax.dot_general(mid, onehot, tn, preferred_element_type=F32)
                + lax.dot_general(lo, onehot, tn, preferred_element_type=F32))
        val = jnp.sum(jnp.where(laneid == col_of, vals, 0.0), axis=0, keepdims=True)
        live = slot[0:1] < cap
        idx_ref[0, pl.ds(e, 1), :] = jnp.where(live, row_of * LANES + col_of, 0.0).astype(jnp.int32)
        val_ref[0, pl.ds(e, 1), :] = jnp.where(live, val, 0.0)
        return carry

    lax.fori_loop(0, ne, per_expert, 0)


def _route(aff4, cap):
    b, ne, nr, _ = aff4.shape
    capp = -(-cap // LANES) * LANES
    blk = pl.BlockSpec((1, ne, nr, LANES), lambda bi: (bi, 0, 0, 0))
    lst = pl.BlockSpec((1, ne, capp), lambda bi: (bi, 0, 0))
    return pl.pallas_call(
        functools.partial(_route_kernel, cap), name="route",
        grid=(b,),
        in_specs=[blk],
        out_specs=[lst, lst, blk],
        out_shape=[jax.ShapeDtypeStruct((b, ne, capp), jnp.int32), jax.ShapeDtypeStruct((b, ne, capp), F32),
                   jax.ShapeDtypeStruct((b, ne, nr, LANES), jnp.int32)],
        scratch_shapes=[pltpu.VMEM((ne, nr, LANES), F32)] * 3,
        compiler_params=_cparams(("arbitrary",)),
    )(aff4)


MOE_SLOTS = 288
MOE_RANGE = 2048
ROW_UNROLL = 8


def _moe_kernel(nrange, nchunk, lo_ref, idx_ref, val_ref, h_ref, wg_ref, wu_ref, wd_ref, o_ref, xg_ref, yb_ref):
    bi, r, e = pl.program_id(0), pl.program_id(1), pl.program_id(2)
    rng = h_ref.shape[1] // nchunk

    @pl.when(e == 0)
    def _():
        o_ref[...] = jnp.zeros_like(o_ref)

    @pl.when((bi == 0) & (r == 0) & (e == 0))
    def _():
        xg_ref[...] = jnp.zeros_like(xg_ref)

    base = (bi * pl.num_programs(2) + e) * (nrange + 1) + r
    lo, hi = lo_ref[base], lo_ref[base + 1]
    tok0 = r * rng

    def tile_of(row):
        return pl.ds(pl.multiple_of(row * nchunk, nchunk), nchunk)

    def chunk(ci, carry):
        j0 = lo + ci * MOE_SLOTS
        cnt = jnp.minimum(MOE_SLOTS, hi - j0)

        def gather(jj, c):
            xg_ref[tile_of(jj), :] = h_ref[0, tile_of(idx_ref[0, 0, j0 + jj] - tok0), :]
            return c

        def gather_group(gi, c):
            j = gi * ROW_UNROLL
            rows = [h_ref[0, tile_of(idx_ref[0, 0, j0 + j + u] - tok0), :] for u in range(ROW_UNROLL)]
            for u in range(ROW_UNROLL):
                xg_ref[tile_of(j + u), :] = rows[u]
            return c

        ngroup = cnt // ROW_UNROLL
        lax.fori_loop(0, ngroup, gather_group, 0)
        lax.fori_loop(ngroup * ROW_UNROLL, cnt, gather, 0)
        xb = _load_row_tiles(xg_ref, (), MOE_SLOTS).astype(BF16)
        hid = _silu(_dot(xb, wg_ref[0])) * _dot(xb, wu_ref[0])
        _store_row_tiles(yb_ref, (), _dot(hid.astype(BF16), wd_ref[0]))

        def scatter(jj, c):
            dst = tile_of(idx_ref[0, 0, j0 + jj] - tok0)
            o_ref[0, dst, :] += val_ref[0, 0, j0 + jj] * yb_ref[tile_of(jj), :]
            return c

        def scatter_group(gi, c):
            j = gi * ROW_UNROLL
            dsts = [tile_of(idx_ref[0, 0, j0 + j + u] - tok0) for u in range(ROW_UNROLL)]
            rows = [o_ref[0, dsts[u], :] + val_ref[0, 0, j0 + j + u] * yb_ref[tile_of(j + u), :]
                    for u in range(ROW_UNROLL)]
            for u in range(ROW_UNROLL):
                o_ref[0, dsts[u], :] = rows[u]
            return c

        lax.fori_loop(0, ngroup, scatter_group, 0)
        lax.fori_loop(ngroup * ROW_UNROLL, cnt, scatter, 0)
        return carry

    lax.fori_loop(0, (hi - lo + MOE_SLOTS - 1) // MOE_SLOTS, chunk, 0)


def _moe(h2, idx, vals, off, cap, wg, wu, wd, layer, rng):
    _, ne, d, f = wg.shape
    nchunk = d // LANES
    b, t = h2.shape[0], h2.shape[1] // nchunk
    nrange = t // rng
    capp = idx.shape[-1]
    lo = off[:, :, ::rng // LANES, 0][:, :, :nrange]
    lo = jnp.concatenate([lo, jnp.full((b, ne, 1), cap, jnp.int32)], axis=-1).reshape(-1)
    lst = pl.BlockSpec((1, 1, capp), lambda bi, r, e, lo_ref: (bi * ne + e, 0, 0), memory_space=pltpu.SMEM)
    tile = pl.BlockSpec((1, rng * nchunk, LANES), lambda bi, r, e, lo_ref: (bi, r, 0))
    return pl.pallas_call(
        functools.partial(_moe_kernel, nrange, nchunk), name="moe",
        grid_spec=pltpu.PrefetchScalarGridSpec(
            num_scalar_prefetch=1,
            grid=(b, nrange, ne),
            in_specs=[lst, lst, tile,
                      pl.BlockSpec((None, 1, d, f), lambda bi, r, e, lo_ref: (layer, e, 0, 0)),
                      pl.BlockSpec((None, 1, d, f), lambda bi, r, e, lo_ref: (layer, e, 0, 0)),
                      pl.BlockSpec((None, 1, f, d), lambda bi, r, e, lo_ref: (layer, e, 0, 0))],
            out_specs=tile,
            scratch_shapes=[pltpu.VMEM((MOE_SLOTS * nchunk, LANES), F32)] * 2),
        out_shape=jax.ShapeDtypeStruct(h2.shape, F32),
        compiler_params=_cparams(("arbitrary", "arbitrary", "arbitrary")),
    )(lo, idx.reshape(b * ne, 1, capp), vals.reshape(b * ne, 1, capp), h2, wg, wu, wd)


def _residual_kernel(x_ref, y_ref, g_ref, o_ref):
    o_ref[0] = x_ref[0] + g_ref[0] * _load_row_tiles(y_ref, (0,), x_ref.shape[1])


def _residual(x, y, gate, tm):
    b, t, d = x.shape
    row = pl.BlockSpec((1, tm, d), lambda bi, i: (bi, i, 0))
    return pl.pallas_call(
        _residual_kernel, name="residual", grid=(b, t // tm),
        in_specs=[row, pl.BlockSpec((1, tm * d // LANES, LANES), lambda bi, i: (bi, i, 0)),
                  pl.BlockSpec((1, 1, d), lambda bi, i: (bi, 0, 0))],
        out_specs=row, out_shape=jax.ShapeDtypeStruct((b, t, d), F32),
        compiler_params=_cparams(("arbitrary", "arbitrary")),
    )(x, y, gate)


def _hgrn_lower_bounds(logits):
    p = jax.nn.softmax(logits.astype(F32), axis=1)
    return jnp.concatenate([jnp.zeros_like(p[:, :1]), jnp.cumsum(p[:, 1:], axis=1)], axis=1)


def _expert_choice_ffn(h2, aff, wg, wu, wd, layer, rng):
    b, ne, t = aff.shape
    cap = EC_CAPACITY_FACTOR * t // ne
    rows = max(t // LANES, 8)
    pad = rows * LANES - t
    if pad:
        aff = jnp.pad(aff, ((0, 0), (0, 0), (0, pad)), constant_values=-1.0)
    idx, vals, off = _route(aff.reshape(b, ne, rows, LANES), cap)
    return _moe(h2, idx, vals, off, cap, wg, wu, wd, layer, rng)


def kernel(x, c, ctx, c_ctx, w_mod, b_mod, norm1_w, w_in, na_q_norm, na_k_norm, na_rpb, hg_lb_logits, hg_norm,
           conv_w, w_out, norm2_w, w_router, w_exp_gate, w_exp_up, w_exp_down):
    b, t, d = x.shape
    wg, wu, wd = w_exp_gate.astype(BF16), w_exp_up.astype(BF16), w_exp_down.astype(BF16)
    depth = w_mod.shape[0]
    ctx_len = ctx.shape[1]
    cond = jnp.zeros((8, d), F32).at[0:b].set(c).at[b].set(c_ctx)
    mod = _modvec(cond, w_mod, b_mod).reshape(depth, 8, N_MOD, d)
    cos, sa, sb = _rope_tables(t)
    cos_c = jnp.ones((ctx_len, LANES), F32)
    zero_c = jnp.zeros((ctx_len, LANES), F32)
    lb = _hgrn_lower_bounds(hg_lb_logits)
    w_in_p = _permute_w_in(w_in)
    w_out_b = w_out.astype(BF16)
    w_router_t = w_router.transpose(0, 2, 1).astype(BF16)
    bias = jax.vmap(lambda r: _na_bias_tiles(r, t // GRID_W))(na_rpb)
    tile2 = lambda v: jnp.tile(v, LANES // HEAD_DIM)[None]

    cx = ctx
    y = y_c = None
    for l in range(depth):
        need_ctx = l < depth - 1
        m_lat = [mod[l, 0:b, j][:, None, :] for j in range(N_MOD)]
        m_ctx = [jnp.broadcast_to(mod[l, b, j][None, None, :], (b, 1, d)) for j in range(N_MOD)]
        g_prev = [mod[l - 1, 0:b, 5][:, None, :], jnp.broadcast_to(mod[l - 1, b, 5][None, None, :], (b, 1, d))] if l else [None, None]
        qn, kn = tile2(na_q_norm[l]), tile2(na_k_norm[l])
        x, qkv, rest = _inproj(x, y, g_prev[0], m_lat[0], m_lat[1], norm1_w[l][None], w_in_p, l, cos, sa, sb, qn, kn, 512)
        cx, qkv_c, rest_c = _inproj(cx, y_c, g_prev[1], m_ctx[0], m_ctx[1], norm1_w[l][None], w_in_p, l,
                                    cos_c, zero_c, zero_c, qn, kn, ctx_len)
        na = _na_attention(qkv, qkv_c, bias, l)
        o = o_c = None
        for rev in (False, True):
            lbd = lb[1 if rev else 0, l][None]
            o, o_c = _gla_scan(rest, rest_c, jnp.log(lbd), jnp.log1p(-lbd), rev, o, o_c,
                               jnp.tile(hg_norm[l], HG_HEADS)[None] if rev else None)
        x, h2, aff = _outproj(na, o, rest, x, w_out_b, l, m_lat[2], m_lat[3], m_lat[4], norm2_w[l][None],
                              conv_w[l], w_router_t[l], 512)
        y = _expert_choice_ffn(h2, aff, wg, wu, wd, l, min(MOE_RANGE, t))
        if need_ctx:
            na_c = _ctx_attention(qkv_c)
            cx, h2_c, aff_c = _outproj(na_c, o_c, rest_c, cx, w_out_b, l, m_ctx[2], m_ctx[3], m_ctx[4],
                                       norm2_w[l][None], conv_w[l], w_router_t[l], ctx_len)
            y_c = _expert_choice_ffn(h2_c, aff_c, wg, wu, wd, l, ctx_len)
    return _residual(x, y, mod[depth - 1, 0:b, 5][:, None, :], 512)
```

```python
import functools

import jax
import jax.numpy as jnp
import numpy as np
from jax import lax
from jax.experimental import pallas as pl
from jax.experimental.pallas import tpu as pltpu

GRID_W = 64
HEAD_DIM = 64
NA_HEADS = 6
NA_DIM = NA_HEADS * HEAD_DIM
NA_ROWS = 8
NA_COLS = 16
HG_HEADS = 6
HG_DIM = HG_HEADS * HEAD_DIM
HG_CHUNK = 64
HG_SUB = 8
CONV_DIM = 256
N_EXPERTS = 16
EC_CAPACITY_FACTOR = 2
ROPE_BASE = 10000.0
NORM_EPS = 1e-6
N_MOD = 6

LANES = 128
NA_QROWS = 4
NA_KROWS = NA_QROWS + NA_ROWS - 1
VMEM_LIMIT = 56 * 1024 * 1024

F32 = jnp.float32
BF16 = jnp.bfloat16
NEG_INF = float("-inf")


def _cparams(sem):
    return pltpu.CompilerParams(dimension_semantics=sem, vmem_limit_bytes=VMEM_LIMIT)


def _dot(a, b):
    return jnp.dot(a, b, preferred_element_type=F32)


def _dot_nt(a, b):
    return lax.dot_general(a, b, (((1,), (1,)), ((), ())), preferred_element_type=F32)


def _split3(x):
    hi = x.astype(BF16)
    r1 = x - hi.astype(F32)
    mid = r1.astype(BF16)
    lo = (r1 - mid.astype(F32)).astype(BF16)
    return hi, mid, lo


def _dot_sel_rhs(sel, x):
    hi, mid, lo = _split3(x)
    return _dot(sel, hi) + _dot(sel, mid) + _dot(sel, lo)


def _dot_sel_lhs(x, sel):
    hi, mid, lo = _split3(x)
    return _dot(hi, sel) + _dot(mid, sel) + _dot(lo, sel)


def _head_sumsq(x, ones):
    sq = x * x
    hi = sq.astype(BF16)
    lo = (sq - hi.astype(F32)).astype(BF16)
    return _dot(hi, ones) + _dot(lo, ones)


def _head_ones():
    r = lax.broadcasted_iota(jnp.int32, (LANES, LANES), 0) // HEAD_DIM
    c = lax.broadcasted_iota(jnp.int32, (LANES, LANES), 1) // HEAD_DIM
    return (r == c).astype(BF16)


def _silu(x):
    return x * (1.0 / (1.0 + jnp.exp(-x)))


def _load_row_tiles(ref, lead, nrows):
    nchunk = ref.shape[-2] // nrows
    return jnp.concatenate([ref[lead + (pl.ds(c, nrows, stride=nchunk), slice(None))] for c in range(nchunk)],
                           axis=1)


def _store_row_tiles(ref, lead, x):
    nrows = x.shape[0]
    nchunk = x.shape[1] // LANES
    for c in range(nchunk):
        ref[lead + (pl.ds(c, nrows, stride=nchunk), slice(None))] = x[:, c * LANES:(c + 1) * LANES]


def _modvec_kernel(cond_ref, w_ref, b_ref, o_ref):
    a = _silu(cond_ref[...]).astype(BF16)
    o_ref[0] = _dot(a, w_ref[0].astype(BF16)) + b_ref[0]


def _modvec(cond, w_mod, b_mod):
    depth, d, n = w_mod.shape
    tn = 512
    return pl.pallas_call(
        _modvec_kernel, name="modvec",
        grid=(depth, n // tn),
        in_specs=[
            pl.BlockSpec((8, d), lambda l, j: (0, 0)),
            pl.BlockSpec((1, d, tn), lambda l, j: (l, 0, j)),
            pl.BlockSpec((1, 1, tn), lambda l, j: (l, 0, j)),
        ],
        out_specs=pl.BlockSpec((1, 8, tn), lambda l, j: (l, 0, j)),
        out_shape=jax.ShapeDtypeStruct((depth, 8, n), F32),
        compiler_params=_cparams(("arbitrary", "arbitrary")),
    )(cond, w_mod, b_mod.reshape(depth, 1, n))


QKV_COLS = 3 * NA_DIM
REST_COLS = 3 * CONV_DIM + 5 * HG_DIM


def _rms_rows(x, w):
    ms = jnp.mean(x * x, axis=-1, keepdims=True)
    return x * lax.rsqrt(ms + NORM_EPS) * w


def _inproj_kernel(has_y, *refs):
    if has_y:
        (x_ref, y_ref, g_ref, sh_ref, sc_ref, nw_ref, w_ref, cos_ref, sa_ref, sb_ref, qn_ref, kn_ref,
         xo_ref, qkv_ref, rest_ref) = refs
        x = x_ref[0] + g_ref[0] * _load_row_tiles(y_ref, (0,), x_ref.shape[1])
        xo_ref[0] = x
    else:
        (x_ref, sh_ref, sc_ref, nw_ref, w_ref, cos_ref, sa_ref, sb_ref, qn_ref, kn_ref,
         qkv_ref, rest_ref) = refs
        x = x_ref[0]
    h = _rms_rows(x, nw_ref[...]) * (1.0 + sc_ref[0]) + sh_ref[0]
    hb = h.astype(BF16)
    ones = _head_ones()
    cos = cos_ref[...]
    sa = sa_ref[...]
    sb = sb_ref[...]
    pq = _dot(hb, w_ref[:, 0:QKV_COLS])
    for c in range(2 * NA_DIM // LANES):
        p = pq[:, c * LANES:(c + 1) * LANES]
        ssq = _head_sumsq(p, ones)
        nw = qn_ref[...] if c < NA_DIM // LANES else kn_ref[...]
        pn = p * lax.rsqrt(ssq * (1.0 / HEAD_DIM) + NORM_EPS) * nw
        pr = pn * cos + pltpu.roll(pn, LANES - 16, 1) * sa + pltpu.roll(pn, 16, 1) * sb
        if c < NA_DIM // LANES:
            pr = pr * (HEAD_DIM ** -0.5)
        qkv_ref[0, :, c * LANES:(c + 1) * LANES] = pr.astype(BF16)
    qkv_ref[0, :, 2 * NA_DIM:] = pq[:, 2 * NA_DIM:].astype(BF16)
    rest_ref[0] = _dot(hb, w_ref[:, QKV_COLS:])


def _inproj(x, y, gate, shift, scale, norm_w, w_in, layer, cos, sa, sb, qn, kn, tm):
    b, t, d = x.shape
    has_y = y is not None
    row = pl.BlockSpec((1, tm, d), lambda bi, i: (bi, i, 0))
    vec = pl.BlockSpec((1, 1, d), lambda bi, i: (bi, 0, 0))
    tab = pl.BlockSpec((tm, LANES), lambda bi, i: (i, 0))
    small = pl.BlockSpec((1, LANES), lambda bi, i: (0, 0))
    row_tiles = pl.BlockSpec((1, tm * d // LANES, LANES), lambda bi, i: (bi, i, 0))
    in_specs = [row] + ([row_tiles, vec] if has_y else []) + [
        vec, vec, pl.BlockSpec((1, d), lambda bi, i: (0, 0)),
        pl.BlockSpec((None,) + w_in.shape[1:], lambda bi, i: (layer, 0, 0)), tab, tab, tab, small, small]
    out_specs = ([row] if has_y else []) + [
        pl.BlockSpec((1, tm, QKV_COLS), lambda bi, i: (bi, i, 0)),
        pl.BlockSpec((1, tm, REST_COLS), lambda bi, i: (bi, i, 0))]
    out_shape = ([jax.ShapeDtypeStruct((b, t, d), F32)] if has_y else []) + [
        jax.ShapeDtypeStruct((b, t, QKV_COLS), BF16), jax.ShapeDtypeStruct((b, t, REST_COLS), F32)]
    args = [x] + ([y, gate] if has_y else []) + [shift, scale, norm_w, w_in, cos, sa, sb, qn, kn]
    outs = pl.pallas_call(
        functools.partial(_inproj_kernel, has_y), name="inproj",
        grid=(b, t // tm), in_specs=in_specs, out_specs=out_specs, out_shape=out_shape,
        compiler_params=_cparams(("arbitrary", "arbitrary")),
    )(*args)
    return outs if has_y else [x] + list(outs)


def _rope_tables(n):
    t = jnp.arange(n, dtype=jnp.int32)
    row = (t // GRID_W).astype(F32)
    col = (t % GRID_W).astype(F32)
    n_freq = HEAD_DIM // 4
    inv = ROPE_BASE ** (-jnp.arange(n_freq, dtype=F32) / n_freq)
    ar = row[:, None] * inv
    ac = col[:, None] * inv
    ang = jnp.concatenate([ar, ar, ac, ac], axis=-1)
    cos = jnp.tile(jnp.cos(ang), (1, LANES // HEAD_DIM))
    sin = jnp.tile(jnp.sin(ang), (1, LANES // HEAD_DIM))
    quarter = (np.arange(LANES) // 16) % 2
    sa = sin * jnp.asarray(np.where(quarter == 0, -1.0, 0.0), F32)
    sb = sin * jnp.asarray(np.where(quarter == 1, 1.0, 0.0), F32)
    return cos, sa, sb


def _permute_w_in(w):
    qkv = w[..., :QKV_COLS]
    hg = w[..., QKV_COLS:QKV_COLS + 5 * HG_DIM]
    hg = jnp.concatenate([hg[..., :HG_DIM] * (HEAD_DIM ** -0.5), hg[..., HG_DIM:]], axis=-1)
    cv = w[..., QKV_COLS + 5 * HG_DIM:]
    return jnp.concatenate([qkv, cv, hg], axis=-1).astype(BF16)


def _na_bias_tiles(rpb, rows):
    nh, n_rel = rpb.shape[0], 2 * NA_ROWS - 1
    qc = np.arange(GRID_W)[:, None]
    kc = np.arange(GRID_W)[None, :]
    ws = np.clip(qc - NA_COLS // 2, 0, GRID_W - NA_COLS)
    col_ok = (kc >= ws) & (kc < ws + NA_COLS)
    rel_col = np.clip(kc - qc + NA_COLS - 1, 0, 2 * NA_COLS - 2)
    blocks = jnp.where(jnp.asarray(col_ok), rpb[:, :, rel_col], NEG_INF)
    blocks = jnp.concatenate([blocks, jnp.full((nh, 1, GRID_W, GRID_W), NEG_INF, rpb.dtype)], axis=1)
    qi = np.arange(NA_QROWS)[:, None]
    kj = np.arange(NA_KROWS)[None, :]
    which = []
    for r0 in (0, NA_QROWS, rows - NA_QROWS):
        kb = int(np.clip(r0 - NA_ROWS // 2, 0, rows - NA_KROWS))
        qr, kr = r0 + qi, kb + kj
        rs = np.clip(qr - NA_ROWS // 2, 0, rows - NA_ROWS)
        ok = (kr >= rs) & (kr < rs + NA_ROWS)
        which.append(np.where(ok, kr - qr + NA_ROWS - 1, n_rel))
    which = np.stack(which).astype(np.int32)
    tiles = blocks[:, which]
    return tiles.transpose(1, 0, 2, 4, 3, 5).reshape(3, nh, NA_QROWS * GRID_W, NA_KROWS * GRID_W)


def _softmax_pv(scores, values):
    m = scores[0].max(axis=-1, keepdims=True)
    for s in scores[1:]:
        m = jnp.maximum(m, s.max(axis=-1, keepdims=True))
    acc, den = None, None
    for s, v in zip(scores, values):
        p = jnp.exp(s - m)
        d = p.sum(axis=-1, keepdims=True)
        o = _dot(p.astype(BF16), v)
        acc = o if acc is None else acc + o
        den = d if den is None else den + d
    return acc / den


def _na_kernel(rows, q_ref, k_ref, v_ref, kc_ref, vc_ref, bias_ref, o_ref):
    i = pl.program_id(2)
    kb = jnp.clip(i * NA_QROWS - NA_ROWS // 2, 0, rows - NA_KROWS)
    start = pl.multiple_of(kb * GRID_W, GRID_W)
    q = q_ref[0]
    kt = k_ref[0, pl.ds(start, NA_KROWS * GRID_W), :]
    vt = v_ref[0, pl.ds(start, NA_KROWS * GRID_W), :]
    kc = kc_ref[0]
    vc = vc_ref[0]
    lane = lax.broadcasted_iota(jnp.int32, q.shape, 1)
    outs = []
    for hh in range(LANES // HEAD_DIM):
        qh = jnp.where((lane // HEAD_DIM) == hh, q, jnp.zeros_like(q))
        s_loc = _dot_nt(qh, kt) + bias_ref[0, hh]
        s_ctx = _dot_nt(qh, kc)
        outs.append(_softmax_pv([s_loc, s_ctx], [vt, vc]))
    o_ref[0] = jnp.where(lane < HEAD_DIM, outs[0], outs[1])


def _na_attention(qkv, qkv_c, bias, layer):
    b, t, _ = qkv.shape
    ctx_len = qkv_c.shape[1]
    rows = t // GRID_W
    nq, nk = NA_QROWS * GRID_W, NA_KROWS * GRID_W
    nt = t // nq
    nhp = NA_DIM // LANES

    def variant(i):
        return jnp.where(i == 0, 0, jnp.where(i == nt - 1, 2, 1))

    return pl.pallas_call(
        functools.partial(_na_kernel, rows), name="na_attn",
        grid=(b, nhp, nt),
        in_specs=[
            pl.BlockSpec((1, nq, LANES), lambda bi, hp, i: (bi, i, hp)),
            pl.BlockSpec((1, t, LANES), lambda bi, hp, i: (bi, 0, nhp + hp)),
            pl.BlockSpec((1, t, LANES), lambda bi, hp, i: (bi, 0, 2 * nhp + hp)),
            pl.BlockSpec((1, ctx_len, LANES), lambda bi, hp, i: (bi, 0, nhp + hp)),
            pl.BlockSpec((1, ctx_len, LANES), lambda bi, hp, i: (bi, 0, 2 * nhp + hp)),
            pl.BlockSpec((None, 1, LANES // HEAD_DIM, nq, nk), lambda bi, hp, i: (layer, variant(i), hp, 0, 0)),
        ],
        out_specs=pl.BlockSpec((1, nq, LANES), lambda bi, hp, i: (bi, i, hp)),
        out_shape=jax.ShapeDtypeStruct((b, t, NA_DIM), F32),
        compiler_params=_cparams(("arbitrary", "arbitrary", "arbitrary")),
    )(qkv, qkv, qkv, qkv_c, qkv_c, bias)


def _ctx_attn_kernel(q_ref, k_ref, v_ref, o_ref):
    q = q_ref[0]
    lane = lax.broadcasted_iota(jnp.int32, q.shape, 1)
    outs = []
    for hh in range(LANES // HEAD_DIM):
        qh = jnp.where((lane // HEAD_DIM) == hh, q, jnp.zeros_like(q))
        outs.append(_softmax_pv([_dot_nt(qh, k_ref[0])], [v_ref[0]]))
    o_ref[0] = jnp.where(lane < HEAD_DIM, outs[0], outs[1])


def _ctx_attention(qkv_c):
    b, n, _ = qkv_c.shape
    nhp = NA_DIM // LANES
    return pl.pallas_call(
        _ctx_attn_kernel, name="ctx_attn",
        grid=(b, nhp),
        in_specs=[pl.BlockSpec((1, n, LANES), lambda bi, hp: (bi, 0, hp)),
                  pl.BlockSpec((1, n, LANES), lambda bi, hp: (bi, 0, nhp + hp)),
                  pl.BlockSpec((1, n, LANES), lambda bi, hp: (bi, 0, 2 * nhp + hp))],
        out_specs=pl.BlockSpec((1, n, LANES), lambda bi, hp: (bi, 0, hp)),
        out_shape=jax.ShapeDtypeStruct((b, n, NA_DIM), F32),
        compiler_params=_cparams(("arbitrary", "arbitrary")),
    )(qkv_c, qkv_c, qkv_c)


REST_HG_BLOCK = {"q": 2, "f_fwd": 3, "f_bwd": 4, "i": 5, "g": 6}


LOG2E = 1.4426950408889634
GLA_GROUP = 4


def _gla_gates(z, lbl, l1m):
    soft = jnp.log1p(jnp.exp(-jnp.abs(z)))
    y = l1m + (jnp.minimum(z, 0.0) - soft)
    hi = jnp.maximum(lbl, y)
    lo = jnp.minimum(lbl, y)
    log_f = hi + jnp.log1p(jnp.exp(lo - hi))
    log_k = l1m + (jnp.minimum(-z, 0.0) - soft)
    return log_f * LOG2E, log_k * LOG2E


def _gla_chunk_stages(q, lk, g, v, rev):
    n, sub, tile = HG_CHUNK, HG_SUB, 8
    width = q.shape[1]
    nheads = width // HEAD_DIM
    pairs = [slice(p * LANES, (p + 1) * LANES) for p in range(width // LANES)]
    head_of_lane = lax.broadcasted_iota(jnp.int32, (sub, width), 1) // HEAD_DIM
    trow = lax.broadcasted_iota(jnp.int32, (tile, width), 0)
    ti = lax.broadcasted_iota(jnp.int32, (n, n), 0)
    si = lax.broadcasted_iota(jnp.int32, (n, n), 1)
    tri = ((ti <= si) if rev else (ti >= si)).astype(BF16)
    a = _dot_sel_rhs(tri, g)
    b = a - lk
    vb = v.astype(BF16)

    qe = (q * jnp.exp2(a)).astype(BF16)
    a_end = a[0:1] if rev else a[n - 1:n]
    kd_end = jnp.exp2(a_end - b).astype(BF16)
    decay = jnp.exp2(a_end)
    bi = lax.broadcasted_iota(jnp.int32, (LANES, LANES), 0) // HEAD_DIM
    bj = lax.broadcasted_iota(jnp.int32, (LANES, LANES), 1) // HEAD_DIM
    tn = (((0,), (0,)), ((), ()))
    grow = [jnp.where(bi == bj, lax.dot_general(vb[:, p], kd_end[:, p], tn, preferred_element_type=F32), 0.0)
            for p in pairs]

    def state(st):
        o_state = jnp.concatenate([_dot_nt(qe[:, p], st[j].astype(BF16)) for j, p in enumerate(pairs)], axis=1)
        return o_state, [st[j] * decay[:, p] + grow[j] for j, p in enumerate(pairs)]

    ones = _head_ones()

    def far_scores(i):
        lo = i * sub
        other = slice(lo + sub, n) if rev else slice(0, lo)
        if other.stop == other.start:
            return None
        r = a[lo + sub - 1:lo + sub] if rev else a[lo:lo + 1]
        qd = q[lo:lo + sub] * jnp.exp2(a[lo:lo + sub] - r)
        kd = jnp.exp2(r - b[other]).astype(BF16)
        stacked = jnp.concatenate([jnp.where(head_of_lane == h, qd, 0.0) for h in range(nheads)], axis=0)
        return other, _dot_nt(stacked.astype(BF16), kd)

    def far_values(far):
        if far is None:
            return None
        other, scores = far
        return _dot(scores.astype(BF16), vb[other])

    def far_pick(o_all):
        pick = o_all[0:sub]
        for h in range(1, nheads):
            pick = jnp.where(head_of_lane == h, o_all[h * sub:(h + 1) * sub], pick)
        return pick

    def near_scores(i):
        lo = i * sub
        prods, pieces = [], []
        for sl in range(sub):
            s = lo + sl
            for h in range(sub // tile):
                t0 = h * tile
                full = (t0 + tile - 1 <= sl) if rev else (t0 >= sl)
                skip = (t0 > sl) if rev else (t0 + tile - 1 < sl)
                if skip:
                    continue
                rows = slice(lo + t0, lo + t0 + tile)
                d = a[rows] - b[s:s + 1]
                if not full:
                    keep = (trow + t0 <= sl) if rev else (trow + t0 >= sl)
                    d = jnp.where(keep, d, NEG_INF)
                prods.append(q[rows] * jnp.exp2(d))
                pieces.append((h, sl))
        pr = jnp.concatenate(prods, axis=0).astype(BF16)
        return pieces, jnp.concatenate([_dot(pr[:, p], ones) for p in pairs], axis=1)

    def near_values(i, pieces, sc):
        lo = i * sub
        acc = [None] * (sub // tile)
        for j, (h, sl) in enumerate(pieces):
            term = sc[j * tile:(j + 1) * tile] * v[lo + sl:lo + sl + 1]
            acc[h] = term if acc[h] is None else acc[h] + term
        return jnp.concatenate(acc, axis=0)

    return state, far_scores, far_values, far_pick, near_scores, near_values


def _gla_chunks(chunks, st, rev):
    nblk = HG_CHUNK // HG_SUB
    stages = [_gla_chunk_stages(*c, rev) for c in chunks]
    o_state = []
    for state, *_ in stages:
        o, st = state(st)
        o_state.append(o)
    far = [[s[1](i) for i in range(nblk)] for s in stages]
    near = [[None] * nblk for _ in stages]
    far_o = [[None] * nblk for _ in stages]
    for i in range(nblk):
        for c, s in enumerate(stages):
            near[c][i] = s[4](i)
            far_o[c][i] = s[2](far[c][i])
    outs = []
    for c, s in enumerate(stages):
        parts = []
        for i in range(nblk):
            o_i = s[5](i, *near[c][i])
            parts.append(o_i if far_o[c][i] is None else o_i + s[3](far_o[c][i]))
        outs.append(jnp.concatenate(parts, axis=0) + o_state[c])
    return outs, st


def _gla_kernel(rev, final, *refs):
    if final:
        (q_ref, f_ref, v_ref, qc_ref, fc_ref, vc_ref, lbl_ref, l1m_ref,
         op_ref, g_ref, opc_ref, gc_ref, nw_ref, o_ref, oc_ref, st_ref) = refs
    else:
        (q_ref, f_ref, v_ref, qc_ref, fc_ref, vc_ref, lbl_ref, l1m_ref,
         o_ref, oc_ref, st_ref) = refs
    step = pl.program_id(1)

    def scan_block(qr, fr, vr, outr, prevr, gater):
        nchunk = qr.shape[1] // HG_CHUNK

        def body(j, carry):
            npair = HG_DIM // LANES
            rows, chunks = [], []
            for u in range(GLA_GROUP):
                c = (nchunk - 1 - (j * GLA_GROUP + u)) if rev else j * GLA_GROUP + u
                r = pl.ds(pl.multiple_of(c * HG_CHUNK, HG_CHUNK), HG_CHUNK)
                log_f, log_k = _gla_gates(fr[0, r, :], lbl_ref[...], l1m_ref[...])
                chunks.append((qr[0, r, :], log_k, log_f, vr[0, r, :]))
                rows.append(r)
            outs, st_new = _gla_chunks(chunks, [st_ref[p] for p in range(npair)], rev)
            for p in range(npair):
                st_ref[p] = st_new[p]
            for r, o in zip(rows, outs):
                if final:
                    tot = o + prevr[0, r, :]
                    ms = jnp.concatenate([_head_sumsq(tot[:, p * LANES:(p + 1) * LANES], _head_ones())
                                          for p in range(npair)], axis=1) * (1.0 / HEAD_DIM)
                    o = tot * lax.rsqrt(ms + NORM_EPS) * nw_ref[...] * _silu(gater[0, r, :])
                outr[0, r, :] = o
            return carry

        lax.fori_loop(0, nchunk // GLA_GROUP, body, 0)

    @pl.when(step == 0)
    def _():
        st_ref[...] = jnp.zeros_like(st_ref)
        scan_block(qc_ref, fc_ref, vc_ref, oc_ref, opc_ref if final else None, gc_ref if final else None)

    @pl.when(step > 0)
    def _():
        scan_block(q_ref, f_ref, v_ref, o_ref, op_ref if final else None, g_ref if final else None)


def _gla_scan(rest, rest_c, lbl, l1m, rev, prev=None, prev_c=None, norm_w=None, tb=512):
    b, t, _ = rest.shape
    ctx_len = rest_c.shape[1]
    nblk = t // tb
    final = prev is not None
    fkey = "f_bwd" if rev else "f_fwd"

    def blk(s):
        return (nblk - jnp.maximum(s, 1)) if rev else jnp.maximum(s - 1, 0)

    def lat(key):
        off = REST_HG_BLOCK[key]
        return pl.BlockSpec((1, tb, HG_DIM), lambda bi, s: (bi, blk(s), off))

    def ctx(key):
        off = REST_HG_BLOCK[key]
        return pl.BlockSpec((1, ctx_len, HG_DIM), lambda bi, s: (bi, 0, off))

    vec = pl.BlockSpec((1, HG_DIM), lambda bi, s: (0, 0))
    o_lat = pl.BlockSpec((1, tb, HG_DIM), lambda bi, s: (bi, blk(s), 0))
    o_ctx = pl.BlockSpec((1, ctx_len, HG_DIM), lambda bi, s: (bi, 0, 0))
    in_specs = [lat("q"), lat(fkey), lat("i"), ctx("q"), ctx(fkey), ctx("i"), vec, vec]
    args = [rest, rest, rest, rest_c, rest_c, rest_c, lbl, l1m]
    if final:
        in_specs += [o_lat, lat("g"), o_ctx, ctx("g"), vec]
        args += [prev, rest, prev_c, rest_c, norm_w]
    return pl.pallas_call(
        functools.partial(_gla_kernel, rev, final), name="gla_bwd" if rev else "gla_fwd",
        grid=(b, nblk + 1),
        in_specs=in_specs,
        out_specs=[o_lat, o_ctx],
        out_shape=[jax.ShapeDtypeStruct((b, t, HG_DIM), F32), jax.ShapeDtypeStruct((b, ctx_len, HG_DIM), F32)],
        scratch_shapes=[pltpu.VMEM((HG_DIM // LANES, LANES, LANES), F32)],
        compiler_params=_cparams(("arbitrary", "arbitrary")),
    )(*args)


HALO = 8


def _outproj_kernel(na_ref, hg_ref, cb_ref, cc_ref, cu_ref, ccp_ref, cup_ref, ccn_ref, cun_ref, x_ref,
                    w_ref, g_ref, sh_ref, sc_ref, nw_ref, cw_ref, wr_ref, xo_ref, h_ref, aff_ref):
    i = pl.program_id(1)
    last = pl.num_programs(1) - 1
    z = cc_ref[0] * cu_ref[0]
    tm = z.shape[0]
    row = lax.broadcasted_iota(jnp.int32, z.shape, 0)
    z_before = jnp.where(i > 0, ccp_ref[0, HALO - 1:HALO, :] * cup_ref[0, HALO - 1:HALO, :], 0.0)
    z_after = jnp.where(i < last, ccn_ref[0, 0:1, :] * cun_ref[0, 0:1, :], 0.0)
    zp = jnp.where(row == 0, z_before, pltpu.roll(z, 1, 0))
    zn = jnp.where(row == tm - 1, z_after, pltpu.roll(z, tm - 1, 0))
    cv = cb_ref[0] * (cw_ref[0:1, :] * zp + cw_ref[1:2, :] * z + cw_ref[2:3, :] * zn)
    mix = jnp.concatenate([na_ref[0], hg_ref[0], cv], axis=-1).astype(BF16)
    x = x_ref[0] + g_ref[0] * _dot(mix, w_ref[...])
    xo_ref[0] = x
    h = _rms_rows(x, nw_ref[...]) * (1.0 + sc_ref[0]) + sh_ref[0]
    _store_row_tiles(h_ref, (0,), h)
    logits = _dot_nt(wr_ref[...], h.astype(BF16))
    p = jnp.exp(logits - logits.max(axis=0, keepdims=True))
    aff_ref[0] = p / p.sum(axis=0, keepdims=True)


def _outproj(na, hg, rest, x, w_out, layer, gate, shift, scale, norm_w, conv_w, w_router_t, tm):
    b, t, d = x.shape
    nh = tm // HALO
    row = pl.BlockSpec((1, tm, d), lambda bi, i: (bi, i, 0))
    vec = pl.BlockSpec((1, 1, d), lambda bi, i: (bi, 0, 0))
    mixer = pl.BlockSpec((1, tm, NA_DIM), lambda bi, i: (bi, i, 0))

    def conv(c):
        return pl.BlockSpec((1, tm, CONV_DIM), lambda bi, i: (bi, i, c))

    def before(c):
        return pl.BlockSpec((1, HALO, CONV_DIM), lambda bi, i: (bi, jnp.maximum(i * nh - 1, 0), c))

    def after(c):
        return pl.BlockSpec((1, HALO, CONV_DIM), lambda bi, i: (bi, jnp.minimum((i + 1) * nh, t // HALO - 1), c))

    return pl.pallas_call(
        _outproj_kernel, name="outproj",
        grid=(b, t // tm),
        in_specs=[mixer, mixer, conv(0), conv(1), conv(2), before(1), before(2), after(1), after(2), row,
                  pl.BlockSpec((None,) + w_out.shape[1:], lambda bi, i: (layer, 0, 0)), vec, vec, vec,
                  pl.BlockSpec((1, d), lambda bi, i: (0, 0)),
                  pl.BlockSpec(conv_w.shape, lambda bi, i: (0, 0)),
                  pl.BlockSpec(w_router_t.shape, lambda bi, i: (0, 0))],
        out_specs=[row, pl.BlockSpec((1, tm * d // LANES, LANES), lambda bi, i: (bi, i, 0)),
                   pl.BlockSpec((1, N_EXPERTS, tm), lambda bi, i: (bi, 0, i))],
        out_shape=[jax.ShapeDtypeStruct((b, t, d), F32), jax.ShapeDtypeStruct((b, t * d // LANES, LANES), F32),
                   jax.ShapeDtypeStruct((b, N_EXPERTS, t), F32)],
        compiler_params=_cparams(("arbitrary", "arbitrary")),
    )(na, hg, rest, rest, rest, rest, rest, rest, rest, x, w_out, gate, shift, scale, norm_w, conv_w, w_router_t)


def _route_kernel(cap, rng_rows, scale, aff_ref, idx_ref, val_ref, off_ref, w_ref, tot_ref, roff_ref):
    ne, nr, _ = aff_ref.shape[1:]
    capp = idx_ref.shape[2]
    a = aff_ref[0]
    bits = pltpu.bitcast(a, jnp.int32)

    def count(m):
        return jnp.sum(jnp.sum(m.astype(F32), axis=2, keepdims=True), axis=1, keepdims=True)

    def search(k, thr):
        cand = thr | jnp.left_shift(jnp.int32(1), 30 - k)
        return jnp.where(count(bits >= cand) >= cap, cand, thr)

    thr = lax.fori_loop(0, 31, search, jnp.zeros((ne, 1, 1), jnp.int32))
    gt = bits > thr
    eq = bits == thr
    need = cap - count(gt)

    li = lax.broadcasted_iota(jnp.int32, (LANES, LANES), 0)
    lj = lax.broadcasted_iota(jnp.int32, (LANES, LANES), 1)
    before_lane = (li < lj).astype(BF16)
    upto_lane = (li <= lj).astype(BF16)
    all_lanes = jnp.ones((LANES, LANES), BF16)
    ri = lax.broadcasted_iota(jnp.int32, (nr, nr), 0)
    rj = lax.broadcasted_iota(jnp.int32, (nr, nr), 1)
    before_row = (rj < ri).astype(BF16)

    def prefix(m, lane_mat):
        m2 = m.astype(BF16).reshape(ne * nr, LANES)
        within = _dot(m2, lane_mat).reshape(ne, nr, LANES)
        tot = _dot(m2, all_lanes).reshape(ne, nr, LANES)
        roff = jnp.stack([_dot(before_row, tot[e].astype(BF16)) for e in range(ne)])
        return within, tot, roff

    w_eq, _, r_eq = prefix(eq, before_lane)
    sel = gt | (eq & (w_eq + r_eq < need))
    w_sel, t_sel, r_sel = prefix(sel, upto_lane)
    off_ref[0] = r_sel.astype(jnp.int32)
    w_ref[...] = w_sel
    tot_ref[...] = t_sel
    roff_ref[...] = r_sel

    slot = lax.broadcasted_iota(jnp.int32, (nr, capp), 1).astype(F32)
    rowid = lax.broadcasted_iota(jnp.int32, (nr, capp), 0).astype(F32)
    laneid = lax.broadcasted_iota(jnp.int32, (LANES, capp), 0).astype(F32)
    tn = (((0,), (0,)), ((), ()))

    def per_expert(e, carry):
        roff = roff_ref[e][:, 0:1]
        rincl = roff + tot_ref[e][:, 0:1]
        row_of = jnp.sum((rincl <= slot).astype(F32), axis=0, keepdims=True)
        onehot = rowid == row_of
        base = jnp.sum(jnp.where(onehot, roff, 0.0), axis=0, keepdims=True)
        onehot = onehot.astype(BF16)
        counts = lax.dot_general(w_ref[e].astype(BF16), onehot, tn, preferred_element_type=F32)
        col_of = jnp.sum((counts <= slot[0:1] - base).astype(F32), axis=0, keepdims=True)
        hi, mid, lo = _split3(aff_ref[0, e])
        vals = (lax.dot_general(hi, onehot, tn, preferred_element_type=F32)
                + lax.dot_general(mid, onehot, tn, preferred_element_type=F32)
                + lax.dot_general(lo, onehot, tn, preferred_element_type=F32))
        val = jnp.sum(jnp.where(laneid == col_of, vals, 0.0), axis=0, keepdims=True)
        live = slot[0:1] < cap
        local_row = row_of - jnp.floor(row_of * (1.0 / rng_rows)) * rng_rows
        idx_ref[0, pl.ds(e, 1), :] = jnp.where(live, (local_row * LANES + col_of) * scale, 0.0).astype(jnp.int32)
        val_ref[0, pl.ds(e, 1), :] = jnp.where(live, val, 0.0)
        return carry

    lax.fori_loop(0, ne, per_expert, 0)


def _route(aff4, cap, rng, scale):
    b, ne, nr, _ = aff4.shape
    capp = -(-cap // LANES) * LANES
    blk = pl.BlockSpec((1, ne, nr, LANES), lambda bi: (bi, 0, 0, 0))
    lst = pl.BlockSpec((1, ne, capp), lambda bi: (bi, 0, 0))
    return pl.pallas_call(
        functools.partial(_route_kernel, cap, rng // LANES, scale), name="route",
        grid=(b,),
        in_specs=[blk],
        out_specs=[lst, lst, blk],
        out_shape=[jax.ShapeDtypeStruct((b, ne, capp), jnp.int32), jax.ShapeDtypeStruct((b, ne, capp), F32),
                   jax.ShapeDtypeStruct((b, ne, nr, LANES), jnp.int32)],
        scratch_shapes=[pltpu.VMEM((ne, nr, LANES), F32)] * 3,
        compiler_params=_cparams(("arbitrary",)),
    )(aff4)


MOE_SLOTS = 288
MOE_RANGE = 2048
ROW_UNROLL = 8


def _moe_kernel(nrange, nchunk, lo_ref, idx_ref, val_ref, h_ref, wg_ref, wu_ref, wd_ref, o_ref, xg_ref, yb_ref):
    bi, r, e = pl.program_id(0), pl.program_id(1), pl.program_id(2)
    rng = h_ref.shape[1] // nchunk

    @pl.when(e == 0)
    def _():
        o_ref[...] = jnp.zeros_like(o_ref)

    @pl.when((bi == 0) & (r == 0) & (e == 0))
    def _():
        xg_ref[...] = jnp.zeros_like(xg_ref)

    base = (bi * pl.num_programs(2) + e) * (nrange + 1) + r
    lo, hi = lo_ref[base], lo_ref[base + 1]

    def tile_of(row):
        return pl.ds(pl.multiple_of(row * nchunk, nchunk), nchunk)

    def tile_at(j):
        return pl.ds(pl.multiple_of(idx_ref[0, 0, j], nchunk), nchunk)

    def chunk(ci, carry):
        j0 = lo + ci * MOE_SLOTS
        cnt = jnp.minimum(MOE_SLOTS, hi - j0)

        def gather(jj, c):
            xg_ref[tile_of(jj), :] = h_ref[0, tile_at(j0 + jj), :]
            return c

        def gather_group(gi, c):
            j = gi * ROW_UNROLL
            rows = [h_ref[0, tile_at(j0 + j + u), :] for u in range(ROW_UNROLL)]
            for u in range(ROW_UNROLL):
                xg_ref[tile_of(j + u), :] = rows[u]
            return c

        ngroup = cnt // ROW_UNROLL
        lax.fori_loop(0, ngroup, gather_group, 0)
        lax.fori_loop(ngroup * ROW_UNROLL, cnt, gather, 0)
        xb = _load_row_tiles(xg_ref, (), MOE_SLOTS).astype(BF16)
        hid = _silu(_dot(xb, wg_ref[0])) * _dot(xb, wu_ref[0])
        _store_row_tiles(yb_ref, (), _dot(hid.astype(BF16), wd_ref[0]))

        def scatter(jj, c):
            dst = tile_at(j0 + jj)
            o_ref[0, dst, :] += val_ref[0, 0, j0 + jj] * yb_ref[tile_of(jj), :]
            return c

        def scatter_group(gi, c):
            j = gi * ROW_UNROLL
            dsts = [tile_at(j0 + j + u) for u in range(ROW_UNROLL)]
            rows = [o_ref[0, dsts[u], :] + val_ref[0, 0, j0 + j + u] * yb_ref[tile_of(j + u), :]
                    for u in range(ROW_UNROLL)]
            for u in range(ROW_UNROLL):
                o_ref[0, dsts[u], :] = rows[u]
            return c

        lax.fori_loop(0, ngroup, scatter_group, 0)
        lax.fori_loop(ngroup * ROW_UNROLL, cnt, scatter, 0)
        return carry

    lax.fori_loop(0, (hi - lo + MOE_SLOTS - 1) // MOE_SLOTS, chunk, 0)


def _moe(h2, idx, vals, off, cap, wg, wu, wd, layer, rng):
    _, ne, d, f = wg.shape
    nchunk = d // LANES
    b, t = h2.shape[0], h2.shape[1] // nchunk
    nrange = t // rng
    capp = idx.shape[-1]
    lo = off[:, :, ::rng // LANES, 0][:, :, :nrange]
    lo = jnp.concatenate([lo, jnp.full((b, ne, 1), cap, jnp.int32)], axis=-1).reshape(-1)
    lst = pl.BlockSpec((1, 1, capp), lambda bi, r, e, lo_ref: (bi * ne + e, 0, 0), memory_space=pltpu.SMEM)
    tile = pl.BlockSpec((1, rng * nchunk, LANES), lambda bi, r, e, lo_ref: (bi, r, 0))
    return pl.pallas_call(
        functools.partial(_moe_kernel, nrange, nchunk), name="moe",
        grid_spec=pltpu.PrefetchScalarGridSpec(
            num_scalar_prefetch=1,
            grid=(b, nrange, ne),
            in_specs=[lst, lst, tile,
                      pl.BlockSpec((None, 1, d, f), lambda bi, r, e, lo_ref: (layer, e, 0, 0)),
                      pl.BlockSpec((None, 1, d, f), lambda bi, r, e, lo_ref: (layer, e, 0, 0)),
                      pl.BlockSpec((None, 1, f, d), lambda bi, r, e, lo_ref: (layer, e, 0, 0))],
            out_specs=tile,
            scratch_shapes=[pltpu.VMEM((MOE_SLOTS * nchunk, LANES), F32)] * 2),
        out_shape=jax.ShapeDtypeStruct(h2.shape, F32),
        compiler_params=_cparams(("arbitrary", "arbitrary", "arbitrary")),
    )(lo, idx.reshape(b * ne, 1, capp), vals.reshape(b * ne, 1, capp), h2, wg, wu, wd)


def _residual_kernel(x_ref, y_ref, g_ref, o_ref):
    o_ref[0] = x_ref[0] + g_ref[0] * _load_row_tiles(y_ref, (0,), x_ref.shape[1])


def _residual(x, y, gate, tm):
    b, t, d = x.shape
    row = pl.BlockSpec((1, tm, d), lambda bi, i: (bi, i, 0))
    return pl.pallas_call(
        _residual_kernel, name="residual", grid=(b, t // tm),
        in_specs=[row, pl.BlockSpec((1, tm * d // LANES, LANES), lambda bi, i: (bi, i, 0)),
                  pl.BlockSpec((1, 1, d), lambda bi, i: (bi, 0, 0))],
        out_specs=row, out_shape=jax.ShapeDtypeStruct((b, t, d), F32),
        compiler_params=_cparams(("arbitrary", "arbitrary")),
    )(x, y, gate)


def _hgrn_lower_bounds(logits):
    p = jax.nn.softmax(logits.astype(F32), axis=1)
    return jnp.concatenate([jnp.zeros_like(p[:, :1]), jnp.cumsum(p[:, 1:], axis=1)], axis=1)


def _expert_choice_ffn(h2, aff, wg, wu, wd, layer, rng):
    b, ne, t = aff.shape
    cap = EC_CAPACITY_FACTOR * t // ne
    rows = max(t // LANES, 8)
    pad = rows * LANES - t
    if pad:
        aff = jnp.pad(aff, ((0, 0), (0, 0), (0, pad)), constant_values=-1.0)
    idx, vals, off = _route(aff.reshape(b, ne, rows, LANES), cap, rng, h2.shape[1] // t)
    return _moe(h2, idx, vals, off, cap, wg, wu, wd, layer, rng)


def kernel(x, c, ctx, c_ctx, w_mod, b_mod, norm1_w, w_in, na_q_norm, na_k_norm, na_rpb, hg_lb_logits, hg_norm,
           conv_w, w_out, norm2_w, w_router, w_exp_gate, w_exp_up, w_exp_down):
    b, t, d = x.shape
    wg, wu, wd = w_exp_gate.astype(BF16), w_exp_up.astype(BF16), w_exp_down.astype(BF16)
    depth = w_mod.shape[0]
    ctx_len = ctx.shape[1]
    cond = jnp.zeros((8, d), F32).at[0:b].set(c).at[b].set(c_ctx)
    mod = _modvec(cond, w_mod, b_mod).reshape(depth, 8, N_MOD, d)
    cos, sa, sb = _rope_tables(t)
    cos_c = jnp.ones((ctx_len, LANES), F32)
    zero_c = jnp.zeros((ctx_len, LANES), F32)
    lb = _hgrn_lower_bounds(hg_lb_logits)
    w_in_p = _permute_w_in(w_in)
    w_out_b = w_out.astype(BF16)
    w_router_t = w_router.transpose(0, 2, 1).astype(BF16)
    bias = jax.vmap(lambda r: _na_bias_tiles(r, t // GRID_W))(na_rpb)
    tile2 = lambda v: jnp.tile(v, LANES // HEAD_DIM)[None]

    cx = ctx
    y = y_c = None
    for l in range(depth):
        need_ctx = l < depth - 1
        m_lat = [mod[l, 0:b, j][:, None, :] for j in range(N_MOD)]
        m_ctx = [jnp.broadcast_to(mod[l, b, j][None, None, :], (b, 1, d)) for j in range(N_MOD)]
        g_prev = [mod[l - 1, 0:b, 5][:, None, :], jnp.broadcast_to(mod[l - 1, b, 5][None, None, :], (b, 1, d))] if l else [None, None]
        qn, kn = tile2(na_q_norm[l]), tile2(na_k_norm[l])
        x, qkv, rest = _inproj(x, y, g_prev[0], m_lat[0], m_lat[1], norm1_w[l][None], w_in_p, l, cos, sa, sb, qn, kn, 512)
        cx, qkv_c, rest_c = _inproj(cx, y_c, g_prev[1], m_ctx[0], m_ctx[1], norm1_w[l][None], w_in_p, l,
                                    cos_c, zero_c, zero_c, qn, kn, ctx_len)
        na = _na_attention(qkv, qkv_c, bias, l)
        o = o_c = None
        for rev in (False, True):
            lbd = lb[1 if rev else 0, l][None]
            o, o_c = _gla_scan(rest, rest_c, jnp.log(lbd), jnp.log1p(-lbd), rev, o, o_c,
                               jnp.tile(hg_norm[l], HG_HEADS)[None] if rev else None)
        x, h2, aff = _outproj(na, o, rest, x, w_out_b, l, m_lat[2], m_lat[3], m_lat[4], norm2_w[l][None],
                              conv_w[l], w_router_t[l], 512)
        y = _expert_choice_ffn(h2, aff, wg, wu, wd, l, min(MOE_RANGE, t))
        if need_ctx:
            na_c = _ctx_attention(qkv_c)
            cx, h2_c, aff_c = _outproj(na_c, o_c, rest_c, cx, w_out_b, l, m_ctx[2], m_ctx[3], m_ctx[4],
                                       norm2_w[l][None], conv_w[l], w_router_t[l], ctx_len)
            y_c = _expert_choice_ffn(h2_c, aff_c, wg, wu, wd, l, ctx_len)
    return _residual(x, y, mod[depth - 1, 0:b, 5][:, None, :], 512)
```

```python
import functools

import jax
import jax.numpy as jnp
import numpy as np
from jax import lax
from jax.experimental import pallas as pl
from jax.experimental.pallas import tpu as pltpu

GRID_W = 64
HEAD_DIM = 64
NA_HEADS = 6
NA_DIM = NA_HEADS * HEAD_DIM
NA_ROWS = 8
NA_COLS = 16
HG_HEADS = 6
HG_DIM = HG_HEADS * HEAD_DIM
HG_CHUNK = 64
HG_SUB = 8
CONV_DIM = 256
N_EXPERTS = 16
EC_CAPACITY_FACTOR = 2
ROPE_BASE = 10000.0
NORM_EPS = 1e-6
N_MOD = 6

LANES = 128
NA_QROWS = 4
NA_KROWS = NA_QROWS + NA_ROWS - 1
VMEM_LIMIT = 56 * 1024 * 1024
PROJ_ROWS = 512
SCAN_ROWS = 512

F32 = jnp.float32
BF16 = jnp.bfloat16
NEG_INF = float("-inf")


def _cparams(sem):
    return pltpu.CompilerParams(dimension_semantics=sem, vmem_limit_bytes=VMEM_LIMIT)


def _dot(a, b):
    return jnp.dot(a, b, preferred_element_type=F32)


def _dot_nt(a, b):
    return lax.dot_general(a, b, (((1,), (1,)), ((), ())), preferred_element_type=F32)


def _split3(x):
    hi = x.astype(BF16)
    r1 = x - hi.astype(F32)
    mid = r1.astype(BF16)
    lo = (r1 - mid.astype(F32)).astype(BF16)
    return hi, mid, lo


def _dot_sel_rhs(sel, x):
    hi, mid, lo = _split3(x)
    return _dot(sel, hi) + _dot(sel, mid) + _dot(sel, lo)


def _dot_sel_lhs(x, sel):
    hi, mid, lo = _split3(x)
    return _dot(hi, sel) + _dot(mid, sel) + _dot(lo, sel)


def _head_sumsq(x, ones):
    sq = x * x
    hi = sq.astype(BF16)
    lo = (sq - hi.astype(F32)).astype(BF16)
    return _dot(hi, ones) + _dot(lo, ones)


def _head_ones():
    r = lax.broadcasted_iota(jnp.int32, (LANES, LANES), 0) // HEAD_DIM
    c = lax.broadcasted_iota(jnp.int32, (LANES, LANES), 1) // HEAD_DIM
    return (r == c).astype(BF16)


def _silu(x):
    return x * (1.0 / (1.0 + jnp.exp(-x)))


def _load_row_tiles(ref, lead, nrows):
    nchunk = ref.shape[-2] // nrows
    return jnp.concatenate([ref[lead + (pl.ds(c, nrows, stride=nchunk), slice(None))] for c in range(nchunk)],
                           axis=1)


def _store_row_tiles(ref, lead, x):
    nrows = x.shape[0]
    nchunk = x.shape[1] // LANES
    for c in range(nchunk):
        ref[lead + (pl.ds(c, nrows, stride=nchunk), slice(None))] = x[:, c * LANES:(c + 1) * LANES]


def _modvec_kernel(cond_ref, w_ref, b_ref, o_ref):
    a = _silu(cond_ref[...]).astype(BF16)
    o_ref[0] = _dot(a, w_ref[0].astype(BF16)) + b_ref[0]


def _modvec(cond, w_mod, b_mod):
    depth, d, n = w_mod.shape
    tn = 512
    return pl.pallas_call(
        _modvec_kernel, name="modvec",
        grid=(depth, n // tn),
        in_specs=[
            pl.BlockSpec((8, d), lambda l, j: (0, 0)),
            pl.BlockSpec((1, d, tn), lambda l, j: (l, 0, j)),
            pl.BlockSpec((1, 1, tn), lambda l, j: (l, 0, j)),
        ],
        out_specs=pl.BlockSpec((1, 8, tn), lambda l, j: (l, 0, j)),
        out_shape=jax.ShapeDtypeStruct((depth, 8, n), F32),
        compiler_params=_cparams(("arbitrary", "arbitrary")),
    )(cond, w_mod, b_mod.reshape(depth, 1, n))


QKV_COLS = 3 * NA_DIM
REST_COLS = 3 * CONV_DIM + 5 * HG_DIM


def _rms_rows(x, w):
    ms = jnp.mean(x * x, axis=-1, keepdims=True)
    return x * lax.rsqrt(ms + NORM_EPS) * w


def _inproj_kernel(has_y, *refs):
    if has_y:
        (x_ref, y_ref, g_ref, sh_ref, sc_ref, nw_ref, w_ref, cos_ref, sa_ref, sb_ref, qn_ref, kn_ref,
         xo_ref, qkv_ref, rest_ref) = refs
        x = x_ref[0] + g_ref[0] * _load_row_tiles(y_ref, (0,), x_ref.shape[1])
        xo_ref[0] = x
    else:
        (x_ref, sh_ref, sc_ref, nw_ref, w_ref, cos_ref, sa_ref, sb_ref, qn_ref, kn_ref,
         qkv_ref, rest_ref) = refs
        x = x_ref[0]
    h = _rms_rows(x, nw_ref[...]) * (1.0 + sc_ref[0]) + sh_ref[0]
    hb = h.astype(BF16)
    ones = _head_ones()
    cos = cos_ref[...]
    sa = sa_ref[...]
    sb = sb_ref[...]
    pq = _dot(hb, w_ref[:, 0:QKV_COLS])
    for c in range(2 * NA_DIM // LANES):
        p = pq[:, c * LANES:(c + 1) * LANES]
        ssq = _head_sumsq(p, ones)
        nw = qn_ref[...] if c < NA_DIM // LANES else kn_ref[...]
        pn = p * lax.rsqrt(ssq * (1.0 / HEAD_DIM) + NORM_EPS) * nw
        pr = pn * cos + pltpu.roll(pn, LANES - 16, 1) * sa + pltpu.roll(pn, 16, 1) * sb
        if c < NA_DIM // LANES:
            pr = pr * (HEAD_DIM ** -0.5)
        qkv_ref[0, :, c * LANES:(c + 1) * LANES] = pr.astype(BF16)
    qkv_ref[0, :, 2 * NA_DIM:] = pq[:, 2 * NA_DIM:].astype(BF16)
    rest_ref[0] = _dot(hb, w_ref[:, QKV_COLS:])


def _inproj(x, y, gate, shift, scale, norm_w, w_in, layer, cos, sa, sb, qn, kn, tm):
    b, t, d = x.shape
    has_y = y is not None
    row = pl.BlockSpec((1, tm, d), lambda bi, i: (bi, i, 0))
    vec = pl.BlockSpec((1, 1, d), lambda bi, i: (bi, 0, 0))
    tab = pl.BlockSpec((tm, LANES), lambda bi, i: (i, 0))
    small = pl.BlockSpec((1, LANES), lambda bi, i: (0, 0))
    row_tiles = pl.BlockSpec((1, tm * d // LANES, LANES), lambda bi, i: (bi, i, 0))
    in_specs = [row] + ([row_tiles, vec] if has_y else []) + [
        vec, vec, pl.BlockSpec((1, d), lambda bi, i: (0, 0)),
        pl.BlockSpec((None,) + w_in.shape[1:], lambda bi, i: (layer, 0, 0)), tab, tab, tab, small, small]
    out_specs = ([row] if has_y else []) + [
        pl.BlockSpec((1, tm, QKV_COLS), lambda bi, i: (bi, i, 0)),
        pl.BlockSpec((1, tm, REST_COLS), lambda bi, i: (bi, i, 0))]
    out_shape = ([jax.ShapeDtypeStruct((b, t, d), F32)] if has_y else []) + [
        jax.ShapeDtypeStruct((b, t, QKV_COLS), BF16), jax.ShapeDtypeStruct((b, t, REST_COLS), F32)]
    args = [x] + ([y, gate] if has_y else []) + [shift, scale, norm_w, w_in, cos, sa, sb, qn, kn]
    outs = pl.pallas_call(
        functools.partial(_inproj_kernel, has_y), name="inproj",
        grid=(b, t // tm), in_specs=in_specs, out_specs=out_specs, out_shape=out_shape,
        compiler_params=_cparams(("arbitrary", "arbitrary")),
    )(*args)
    return outs if has_y else [x] + list(outs)


def _rope_tables(n):
    t = jnp.arange(n, dtype=jnp.int32)
    row = (t // GRID_W).astype(F32)
    col = (t % GRID_W).astype(F32)
    n_freq = HEAD_DIM // 4
    inv = ROPE_BASE ** (-jnp.arange(n_freq, dtype=F32) / n_freq)
    ar = row[:, None] * inv
    ac = col[:, None] * inv
    ang = jnp.concatenate([ar, ar, ac, ac], axis=-1)
    cos = jnp.tile(jnp.cos(ang), (1, LANES // HEAD_DIM))
    sin = jnp.tile(jnp.sin(ang), (1, LANES // HEAD_DIM))
    quarter = (np.arange(LANES) // 16) % 2
    sa = sin * jnp.asarray(np.where(quarter == 0, -1.0, 0.0), F32)
    sb = sin * jnp.asarray(np.where(quarter == 1, 1.0, 0.0), F32)
    return cos, sa, sb


def _permute_w_in(w):
    qkv = w[..., :QKV_COLS]
    hg = w[..., QKV_COLS:QKV_COLS + 5 * HG_DIM]
    hg = jnp.concatenate([hg[..., :HG_DIM] * (HEAD_DIM ** -0.5), hg[..., HG_DIM:]], axis=-1)
    cv = w[..., QKV_COLS + 5 * HG_DIM:]
    return jnp.concatenate([qkv, cv, hg], axis=-1).astype(BF16)


def _na_bias_tiles(rpb, rows):
    nh, n_rel = rpb.shape[0], 2 * NA_ROWS - 1
    qc = np.arange(GRID_W)[:, None]
    kc = np.arange(GRID_W)[None, :]
    ws = np.clip(qc - NA_COLS // 2, 0, GRID_W - NA_COLS)
    col_ok = (kc >= ws) & (kc < ws + NA_COLS)
    rel_col = np.clip(kc - qc + NA_COLS - 1, 0, 2 * NA_COLS - 2)
    blocks = jnp.where(jnp.asarray(col_ok), rpb[:, :, rel_col], NEG_INF)
    blocks = jnp.concatenate([blocks, jnp.full((nh, 1, GRID_W, GRID_W), NEG_INF, rpb.dtype)], axis=1)
    qi = np.arange(NA_QROWS)[:, None]
    kj = np.arange(NA_KROWS)[None, :]
    which = []
    for r0 in (0, NA_QROWS, rows - NA_QROWS):
        kb = int(np.clip(r0 - NA_ROWS // 2, 0, rows - NA_KROWS))
        qr, kr = r0 + qi, kb + kj
        rs = np.clip(qr - NA_ROWS // 2, 0, rows - NA_ROWS)
        ok = (kr >= rs) & (kr < rs + NA_ROWS)
        which.append(np.where(ok, kr - qr + NA_ROWS - 1, n_rel))
    which = np.stack(which).astype(np.int32)
    tiles = blocks[:, which]
    return tiles.transpose(1, 0, 2, 4, 3, 5).reshape(3, nh, NA_QROWS * GRID_W, NA_KROWS * GRID_W)


def _softmax_pv(scores, values):
    m = scores[0].max(axis=-1, keepdims=True)
    for s in scores[1:]:
        m = jnp.maximum(m, s.max(axis=-1, keepdims=True))
    acc, den = None, None
    for s, v in zip(scores, values):
        p = jnp.exp(s - m)
        d = p.sum(axis=-1, keepdims=True)
        o = _dot(p.astype(BF16), v)
        acc = o if acc is None else acc + o
        den = d if den is None else den + d
    return acc / den


def _na_kernel(rows, q_ref, k_ref, v_ref, kc_ref, vc_ref, bias_ref, o_ref):
    i = pl.program_id(2)
    kb = jnp.clip(i * NA_QROWS - NA_ROWS // 2, 0, rows - NA_KROWS)
    start = pl.multiple_of(kb * GRID_W, GRID_W)
    q = q_ref[0]
    kt = k_ref[0, pl.ds(start, NA_KROWS * GRID_W), :]
    vt = v_ref[0, pl.ds(start, NA_KROWS * GRID_W), :]
    kc = kc_ref[0]
    vc = vc_ref[0]
    lane = lax.broadcasted_iota(jnp.int32, q.shape, 1)
    outs = []
    for hh in range(LANES // HEAD_DIM):
        qh = jnp.where((lane // HEAD_DIM) == hh, q, jnp.zeros_like(q))
        s_loc = _dot_nt(qh, kt) + bias_ref[0, hh]
        s_ctx = _dot_nt(qh, kc)
        outs.append(_softmax_pv([s_loc, s_ctx], [vt, vc]))
    o_ref[0] = jnp.where(lane < HEAD_DIM, outs[0], outs[1]).astype(o_ref.dtype)


def _na_attention(qkv, qkv_c, bias, layer):
    b, t, _ = qkv.shape
    ctx_len = qkv_c.shape[1]
    rows = t // GRID_W
    nq, nk = NA_QROWS * GRID_W, NA_KROWS * GRID_W
    nt = t // nq
    nhp = NA_DIM // LANES

    def variant(i):
        return jnp.where(i == 0, 0, jnp.where(i == nt - 1, 2, 1))

    return pl.pallas_call(
        functools.partial(_na_kernel, rows), name="na_attn",
        grid=(b, nhp, nt),
        in_specs=[
            pl.BlockSpec((1, nq, LANES), lambda bi, hp, i: (bi, i, hp)),
            pl.BlockSpec((1, t, LANES), lambda bi, hp, i: (bi, 0, nhp + hp)),
            pl.BlockSpec((1, t, LANES), lambda bi, hp, i: (bi, 0, 2 * nhp + hp)),
            pl.BlockSpec((1, ctx_len, LANES), lambda bi, hp, i: (bi, 0, nhp + hp)),
            pl.BlockSpec((1, ctx_len, LANES), lambda bi, hp, i: (bi, 0, 2 * nhp + hp)),
            pl.BlockSpec((None, 1, LANES // HEAD_DIM, nq, nk), lambda bi, hp, i: (layer, variant(i), hp, 0, 0)),
        ],
        out_specs=pl.BlockSpec((1, nq, LANES), lambda bi, hp, i: (bi, i, hp)),
        out_shape=jax.ShapeDtypeStruct((b, t, NA_DIM), BF16),
        compiler_params=_cparams(("arbitrary", "arbitrary", "arbitrary")),
    )(qkv, qkv, qkv, qkv_c, qkv_c, bias)


def _ctx_attn_kernel(q_ref, k_ref, v_ref, o_ref):
    q = q_ref[0]
    lane = lax.broadcasted_iota(jnp.int32, q.shape, 1)
    outs = []
    for hh in range(LANES // HEAD_DIM):
        qh = jnp.where((lane // HEAD_DIM) == hh, q, jnp.zeros_like(q))
        outs.append(_softmax_pv([_dot_nt(qh, k_ref[0])], [v_ref[0]]))
    o_ref[0] = jnp.where(lane < HEAD_DIM, outs[0], outs[1]).astype(o_ref.dtype)


def _ctx_attention(qkv_c):
    b, n, _ = qkv_c.shape
    nhp = NA_DIM // LANES
    return pl.pallas_call(
        _ctx_attn_kernel, name="ctx_attn",
        grid=(b, nhp),
        in_specs=[pl.BlockSpec((1, n, LANES), lambda bi, hp: (bi, 0, hp)),
                  pl.BlockSpec((1, n, LANES), lambda bi, hp: (bi, 0, nhp + hp)),
                  pl.BlockSpec((1, n, LANES), lambda bi, hp: (bi, 0, 2 * nhp + hp))],
        out_specs=pl.BlockSpec((1, n, LANES), lambda bi, hp: (bi, 0, hp)),
        out_shape=jax.ShapeDtypeStruct((b, n, NA_DIM), BF16),
        compiler_params=_cparams(("arbitrary", "arbitrary")),
    )(qkv_c, qkv_c, qkv_c)


REST_HG_BLOCK = {"q": 2, "f_fwd": 3, "f_bwd": 4, "i": 5, "g": 6}


LOG2E = 1.4426950408889634
GLA_GROUP = 4


def _gla_gates(z, lbl, l1m):
    soft = jnp.log1p(jnp.exp(-jnp.abs(z)))
    y = l1m + (jnp.minimum(z, 0.0) - soft)
    hi = jnp.maximum(lbl, y)
    lo = jnp.minimum(lbl, y)
    log_f = hi + jnp.log1p(jnp.exp(lo - hi))
    log_k = l1m + (jnp.minimum(-z, 0.0) - soft)
    return log_f * LOG2E, log_k * LOG2E


def _gla_chunk_stages(q, lk, g, v, rev):
    n, sub, tile = HG_CHUNK, HG_SUB, 8
    width = q.shape[1]
    nheads = width // HEAD_DIM
    pairs = [slice(p * LANES, (p + 1) * LANES) for p in range(width // LANES)]
    head_of_lane = lax.broadcasted_iota(jnp.int32, (sub, width), 1) // HEAD_DIM
    trow = lax.broadcasted_iota(jnp.int32, (tile, width), 0)
    ti = lax.broadcasted_iota(jnp.int32, (n, n), 0)
    si = lax.broadcasted_iota(jnp.int32, (n, n), 1)
    tri = ((ti <= si) if rev else (ti >= si)).astype(BF16)
    a = _dot_sel_rhs(tri, g)
    b = a - lk
    vb = v.astype(BF16)

    qe = (q * jnp.exp2(a)).astype(BF16)
    a_end = a[0:1] if rev else a[n - 1:n]
    kd_end = jnp.exp2(a_end - b).astype(BF16)
    decay = jnp.exp2(a_end)
    bi = lax.broadcasted_iota(jnp.int32, (LANES, LANES), 0) // HEAD_DIM
    bj = lax.broadcasted_iota(jnp.int32, (LANES, LANES), 1) // HEAD_DIM
    tn = (((0,), (0,)), ((), ()))
    grow = [jnp.where(bi == bj, lax.dot_general(vb[:, p], kd_end[:, p], tn, preferred_element_type=F32), 0.0)
            for p in pairs]

    def state(st):
        o_state = jnp.concatenate([_dot_nt(qe[:, p], st[j].astype(BF16)) for j, p in enumerate(pairs)], axis=1)
        return o_state, [st[j] * decay[:, p] + grow[j] for j, p in enumerate(pairs)]

    ones = _head_ones()

    def far_scores(i):
        lo = i * sub
        other = slice(lo + sub, n) if rev else slice(0, lo)
        if other.stop == other.start:
            return None
        r = a[lo + sub - 1:lo + sub] if rev else a[lo:lo + 1]
        qd = q[lo:lo + sub] * jnp.exp2(a[lo:lo + sub] - r)
        kd = jnp.exp2(r - b[other]).astype(BF16)
        stacked = jnp.concatenate([jnp.where(head_of_lane == h, qd, 0.0) for h in range(nheads)], axis=0)
        return other, _dot_nt(stacked.astype(BF16), kd)

    def far_values(far):
        if far is None:
            return None
        other, scores = far
        return _dot(scores.astype(BF16), vb[other])

    def far_pick(o_all):
        pick = o_all[0:sub]
        for h in range(1, nheads):
            pick = jnp.where(head_of_lane == h, o_all[h * sub:(h + 1) * sub], pick)
        return pick

    def near_scores(i):
        lo = i * sub
        prods, pieces = [], []
        for sl in range(sub):
            s = lo + sl
            for h in range(sub // tile):
                t0 = h * tile
                full = (t0 + tile - 1 <= sl) if rev else (t0 >= sl)
                skip = (t0 > sl) if rev else (t0 + tile - 1 < sl)
                if skip:
                    continue
                rows = slice(lo + t0, lo + t0 + tile)
                d = a[rows] - b[s:s + 1]
                if not full:
                    keep = (trow + t0 <= sl) if rev else (trow + t0 >= sl)
                    d = jnp.where(keep, d, NEG_INF)
                prods.append(q[rows] * jnp.exp2(d))
                pieces.append((h, sl))
        pr = jnp.concatenate(prods, axis=0).astype(BF16)
        return pieces, jnp.concatenate([_dot(pr[:, p], ones) for p in pairs], axis=1)

    def near_values(i, pieces, sc):
        lo = i * sub
        acc = [None] * (sub // tile)
        for j, (h, sl) in enumerate(pieces):
            term = sc[j * tile:(j + 1) * tile] * v[lo + sl:lo + sl + 1]
            acc[h] = term if acc[h] is None else acc[h] + term
        return jnp.concatenate(acc, axis=0)

    return state, far_scores, far_values, far_pick, near_scores, near_values


def _gla_chunks(chunks, st, rev):
    nblk = HG_CHUNK // HG_SUB
    stages = [_gla_chunk_stages(*c, rev) for c in chunks]
    o_state = []
    for state, *_ in stages:
        o, st = state(st)
        o_state.append(o)
    far = [[s[1](i) for i in range(nblk)] for s in stages]
    near = [[None] * nblk for _ in stages]
    far_o = [[None] * nblk for _ in stages]
    for i in range(nblk):
        for c, s in enumerate(stages):
            near[c][i] = s[4](i)
            far_o[c][i] = s[2](far[c][i])
    outs = []
    for c, s in enumerate(stages):
        parts = []
        for i in range(nblk):
            o_i = s[5](i, *near[c][i])
            parts.append(o_i if far_o[c][i] is None else o_i + s[3](far_o[c][i]))
        outs.append(jnp.concatenate(parts, axis=0) + o_state[c])
    return outs, st


def _gla_kernel(rev, final, *refs):
    if final:
        (q_ref, f_ref, v_ref, qc_ref, fc_ref, vc_ref, lbl_ref, l1m_ref,
         op_ref, g_ref, opc_ref, gc_ref, nw_ref, o_ref, oc_ref, st_ref) = refs
    else:
        (q_ref, f_ref, v_ref, qc_ref, fc_ref, vc_ref, lbl_ref, l1m_ref,
         o_ref, oc_ref, st_ref) = refs
    step = pl.program_id(1)

    def scan_block(qr, fr, vr, outr, prevr, gater):
        nchunk = qr.shape[1] // HG_CHUNK

        def body(j, carry):
            npair = HG_DIM // LANES
            rows, chunks = [], []
            for u in range(GLA_GROUP):
                c = (nchunk - 1 - (j * GLA_GROUP + u)) if rev else j * GLA_GROUP + u
                r = pl.ds(pl.multiple_of(c * HG_CHUNK, HG_CHUNK), HG_CHUNK)
                log_f, log_k = _gla_gates(fr[0, r, :], lbl_ref[...], l1m_ref[...])
                chunks.append((qr[0, r, :], log_k, log_f, vr[0, r, :]))
                rows.append(r)
            outs, st_new = _gla_chunks(chunks, [st_ref[p] for p in range(npair)], rev)
            for p in range(npair):
                st_ref[p] = st_new[p]
            for r, o in zip(rows, outs):
                if final:
                    tot = o + prevr[0, r, :]
                    ms = jnp.concatenate([_head_sumsq(tot[:, p * LANES:(p + 1) * LANES], _head_ones())
                                          for p in range(npair)], axis=1) * (1.0 / HEAD_DIM)
                    o = tot * lax.rsqrt(ms + NORM_EPS) * nw_ref[...] * _silu(gater[0, r, :])
                outr[0, r, :] = o.astype(outr.dtype)
            return carry

        lax.fori_loop(0, nchunk // GLA_GROUP, body, 0)

    @pl.when(step == 0)
    def _():
        st_ref[...] = jnp.zeros_like(st_ref)
        scan_block(qc_ref, fc_ref, vc_ref, oc_ref, opc_ref if final else None, gc_ref if final else None)

    @pl.when(step > 0)
    def _():
        scan_block(q_ref, f_ref, v_ref, o_ref, op_ref if final else None, g_ref if final else None)


def _gla_scan(rest, rest_c, lbl, l1m, rev, prev=None, prev_c=None, norm_w=None, tb=HG_CHUNK):
    b, t, _ = rest.shape
    ctx_len = rest_c.shape[1]
    nblk = t // tb
    final = prev is not None
    out_dtype = BF16 if final else F32
    fkey ="f_bwd" if rev else "f_fwd"

    def blk(s):
        return (nblk - jnp.maximum(s, 1)) if rev else jnp.maximum(s - 1, 0)

    def lat(key):
        off = REST_HG_BLOCK[key]
        return pl.BlockSpec((1, tb, HG_DIM), lambda bi, s: (bi, blk(s), off))

    def ctx(key):
        off = REST_HG_BLOCK[key]
        return pl.BlockSpec((1, ctx_len, HG_DIM), lambda bi, s: (bi, 0, off))

    vec = pl.BlockSpec((1, HG_DIM), lambda bi, s: (0, 0))
    o_lat = pl.BlockSpec((1, tb, HG_DIM), lambda bi, s: (bi, blk(s), 0))
    o_ctx = pl.BlockSpec((1, ctx_len, HG_DIM), lambda bi, s: (bi, 0, 0))
    in_specs = [lat("q"), lat(fkey), lat("i"), ctx("q"), ctx(fkey), ctx("i"), vec, vec]
    args = [rest, rest, rest, rest_c, rest_c, rest_c, lbl, l1m]
    if final:
        in_specs += [o_lat, lat("g"), o_ctx, ctx("g"), vec]
        args += [prev, rest, prev_c, rest_c, norm_w]
    return pl.pallas_call(
        functools.partial(_gla_kernel, rev, final), name="gla_bwd" if rev else "gla_fwd",
        grid=(b, nblk + 1),
        in_specs=in_specs,
        out_specs=[o_lat, o_ctx],
        out_shape=[jax.ShapeDtypeStruct((b, t, HG_DIM), out_dtype), jax.ShapeDtypeStruct((b, ctx_len, HG_DIM), out_dtype)],
        scratch_shapes=[pltpu.VMEM((HG_DIM // LANES, LANES, LANES), F32)],
        compiler_params=_cparams(("arbitrary", "arbitrary")),
    )(*args)


HALO = 8


def _outproj_kernel(na_ref, hg_ref, cb_ref, cc_ref, cu_ref, ccp_ref, cup_ref, ccn_ref, cun_ref, x_ref,
                    w_ref, g_ref, sh_ref, sc_ref, nw_ref, cw_ref, wr_ref, xo_ref, h_ref, aff_ref):
    i = pl.program_id(1)
    last = pl.num_programs(1) - 1
    z = cc_ref[0] * cu_ref[0]
    tm = z.shape[0]
    row = lax.broadcasted_iota(jnp.int32, z.shape, 0)
    z_before = jnp.where(i > 0, ccp_ref[0, HALO - 1:HALO, :] * cup_ref[0, HALO - 1:HALO, :], 0.0)
    z_after = jnp.where(i < last, ccn_ref[0, 0:1, :] * cun_ref[0, 0:1, :], 0.0)
    zp = jnp.where(row == 0, z_before, pltpu.roll(z, 1, 0))
    zn = jnp.where(row == tm - 1, z_after, pltpu.roll(z, tm - 1, 0))
    cv = cb_ref[0] * (cw_ref[0:1, :] * zp + cw_ref[1:2, :] * z + cw_ref[2:3, :] * zn)
    mix = jnp.concatenate([na_ref[0], hg_ref[0], cv.astype(BF16)], axis=-1)
    x = x_ref[0] + g_ref[0] * _dot(mix, w_ref[...])
    xo_ref[0] = x
    h = _rms_rows(x, nw_ref[...]) * (1.0 + sc_ref[0]) + sh_ref[0]
    _store_row_tiles(h_ref, (0,), h)
    logits = _dot_nt(wr_ref[...], h.astype(BF16))
    p = jnp.exp(logits - logits.max(axis=0, keepdims=True))
    aff_ref[0] = p / p.sum(axis=0, keepdims=True)


def _outproj(na, hg, rest, x, w_out, layer, gate, shift, scale, norm_w, conv_w, w_router_t, tm):
    b, t, d = x.shape
    nh = tm // HALO
    row = pl.BlockSpec((1, tm, d), lambda bi, i: (bi, i, 0))
    vec = pl.BlockSpec((1, 1, d), lambda bi, i: (bi, 0, 0))
    mixer = pl.BlockSpec((1, tm, NA_DIM), lambda bi, i: (bi, i, 0))

    def conv(c):
        return pl.BlockSpec((1, tm, CONV_DIM), lambda bi, i: (bi, i, c))

    def before(c):
        return pl.BlockSpec((1, HALO, CONV_DIM), lambda bi, i: (bi, jnp.maximum(i * nh - 1, 0), c))

    def after(c):
        return pl.BlockSpec((1, HALO, CONV_DIM), lambda bi, i: (bi, jnp.minimum((i + 1) * nh, t // HALO - 1), c))

    return pl.pallas_call(
        _outproj_kernel, name="outproj",
        grid=(b, t // tm),
        in_specs=[mixer, mixer, conv(0), conv(1), conv(2), before(1), before(2), after(1), after(2), row,
                  pl.BlockSpec((None,) + w_out.shape[1:], lambda bi, i: (layer, 0, 0)), vec, vec, vec,
                  pl.BlockSpec((1, d), lambda bi, i: (0, 0)),
                  pl.BlockSpec(conv_w.shape, lambda bi, i: (0, 0)),
                  pl.BlockSpec(w_router_t.shape, lambda bi, i: (0, 0))],
        out_specs=[row, pl.BlockSpec((1, tm * d // LANES, LANES), lambda bi, i: (bi, i, 0)),
                   pl.BlockSpec((1, N_EXPERTS, tm), lambda bi, i: (bi, 0, i))],
        out_shape=[jax.ShapeDtypeStruct((b, t, d), F32), jax.ShapeDtypeStruct((b, t * d // LANES, LANES), F32),
                   jax.ShapeDtypeStruct((b, N_EXPERTS, t), F32)],
        compiler_params=_cparams(("arbitrary", "arbitrary")),
    )(na, hg, rest, rest, rest, rest, rest, rest, rest, x, w_out, gate, shift, scale, norm_w, conv_w, w_router_t)


def _route_kernel(cap, rng_rows, scale, aff_ref, idx_ref, val_ref, off_ref, w_ref, tot_ref, roff_ref):
    ne, nr, _ = aff_ref.shape[1:]
    capp = idx_ref.shape[2]
    a = aff_ref[0]
    bits = pltpu.bitcast(a, jnp.int32)

    def count(m):
        return jnp.sum(jnp.sum(m.astype(F32), axis=2, keepdims=True), axis=1, keepdims=True)

    def search(k, thr):
        cand = thr | jnp.left_shift(jnp.int32(1), 30 - k)
        return jnp.where(count(bits >= cand) >= cap, cand, thr)

    thr = lax.fori_loop(0, 31, search, jnp.zeros((ne, 1, 1), jnp.int32))
    gt = bits > thr
    eq = bits == thr
    need = cap - count(gt)

    li = lax.broadcasted_iota(jnp.int32, (LANES, LANES), 0)
    lj = lax.broadcasted_iota(jnp.int32, (LANES, LANES), 1)
    before_lane = (li < lj).astype(BF16)
    upto_lane = (li <= lj).astype(BF16)
    all_lanes = jnp.ones((LANES, LANES), BF16)
    ri = lax.broadcasted_iota(jnp.int32, (nr, nr), 0)
    rj = lax.broadcasted_iota(jnp.int32, (nr, nr), 1)
    before_row = (rj < ri).astype(BF16)

    def prefix(m, lane_mat):
        m2 = m.astype(BF16).reshape(ne * nr, LANES)
        within = _dot(m2, lane_mat).reshape(ne, nr, LANES)
        tot = _dot(m2, all_lanes).reshape(ne, nr, LANES)
        roff = jnp.stack([_dot(before_row, tot[e].astype(BF16)) for e in range(ne)])
        return within, tot, roff

    w_eq, _, r_eq = prefix(eq, before_lane)
    sel = gt | (eq & (w_eq + r_eq < need))
    w_sel, t_sel, r_sel = prefix(sel, upto_lane)
    off_ref[0] = r_sel.astype(jnp.int32)
    w_ref[...] = w_sel
    tot_ref[...] = t_sel
    roff_ref[...] = r_sel

    slot = lax.broadcasted_iota(jnp.int32, (nr, capp), 1).astype(F32)
    rowid = lax.broadcasted_iota(jnp.int32, (nr, capp), 0).astype(F32)
    laneid = lax.broadcasted_iota(jnp.int32, (LANES, capp), 0).astype(F32)
    tn = (((0,), (0,)), ((), ()))

    def per_expert(e, carry):
        roff = roff_ref[e][:, 0:1]
        rincl = roff + tot_ref[e][:, 0:1]
        row_of = jnp.sum((rincl <= slot).astype(F32), axis=0, keepdims=True)
        onehot = rowid == row_of
        base = jnp.sum(jnp.where(onehot, roff, 0.0), axis=0, keepdims=True)
        onehot = onehot.astype(BF16)
        counts = lax.dot_general(w_ref[e].astype(BF16), onehot, tn, preferred_element_type=F32)
        col_of = jnp.sum((counts <= slot[0:1] - base).astype(F32), axis=0, keepdims=True)
        hi, mid, lo = _split3(aff_ref[0, e])
        vals = (lax.dot_general(hi, onehot, tn, preferred_element_type=F32)
                + lax.dot_general(mid, onehot, tn, preferred_element_type=F32)
                + lax.dot_general(lo, onehot, tn, preferred_element_type=F32))
        val = jnp.sum(jnp.where(laneid == col_of, vals, 0.0), axis=0, keepdims=True)
        live = slot[0:1] < cap
        local_row = row_of - jnp.floor(row_of * (1.0 / rng_rows)) * rng_rows
        idx_ref[0, pl.ds(e, 1), :] = jnp.where(live, (local_row * LANES + col_of) * scale, 0.0).astype(jnp.int32)
        val_ref[0, pl.ds(e, 1), :] = jnp.where(live, val, 0.0)
        return carry

    lax.fori_loop(0, ne, per_expert, 0)


def _route(aff4, cap, rng, scale):
    b, ne, nr, _ = aff4.shape
    capp = -(-cap // LANES) * LANES
    blk = pl.BlockSpec((1, ne, nr, LANES), lambda bi: (bi, 0, 0, 0))
    lst = pl.BlockSpec((1, ne, capp), lambda bi: (bi, 0, 0))
    return pl.pallas_call(
        functools.partial(_route_kernel, cap, rng // LANES, scale), name="route",
        grid=(b,),
        in_specs=[blk],
        out_specs=[lst, lst, blk],
        out_shape=[jax.ShapeDtypeStruct((b, ne, capp), jnp.int32), jax.ShapeDtypeStruct((b, ne, capp), F32),
                   jax.ShapeDtypeStruct((b, ne, nr, LANES), jnp.int32)],
        scratch_shapes=[pltpu.VMEM((ne, nr, LANES), F32)] * 3,
        compiler_params=_cparams(("arbitrary",)),
    )(aff4)


MOE_SLOTS = 288
MOE_RANGE = 2048
ROW_UNROLL = 8


def _moe_kernel(nrange, nchunk, slots, lo_ref, idx_ref, val_ref, h_ref, wg_ref, wu_ref, wd_ref, o_ref, xg_ref, yb_ref):
    bi, r, e = pl.program_id(0), pl.program_id(1), pl.program_id(2)

    @pl.when(e == 0)
    def _():
        o_ref[...] = jnp.zeros_like(o_ref)

    @pl.when((bi == 0) & (r == 0) & (e == 0))
    def _():
        xg_ref[...] = jnp.zeros_like(xg_ref)

    base = (bi * pl.num_programs(2) + e) * (nrange + 1) + r
    lo, hi = lo_ref[base], lo_ref[base + 1]

    def tile_of(row):
        return pl.ds(pl.multiple_of(row * nchunk, nchunk), nchunk)

    def tile_at(j):
        return pl.ds(pl.multiple_of(idx_ref[0, 0, j], nchunk), nchunk)

    def chunk(ci, carry):
        j0 = lo + ci * slots
        cnt = jnp.minimum(slots, hi - j0)

        def gather(jj, c):
            xg_ref[tile_of(jj), :] = h_ref[0, tile_at(j0 + jj), :]
            return c

        def gather_group(gi, c):
            j = gi * ROW_UNROLL
            rows = [h_ref[0, tile_at(j0 + j + u), :] for u in range(ROW_UNROLL)]
            for u in range(ROW_UNROLL):
                xg_ref[tile_of(j + u), :] = rows[u]
            return c

        ngroup = cnt // ROW_UNROLL
        lax.fori_loop(0, ngroup, gather_group, 0)
        lax.fori_loop(ngroup * ROW_UNROLL, cnt, gather, 0)
        xb = _load_row_tiles(xg_ref, (), slots).astype(BF16)
        hid = _silu(_dot(xb, wg_ref[0])) * _dot(xb, wu_ref[0])
        _store_row_tiles(yb_ref, (), _dot(hid.astype(BF16), wd_ref[0]))

        def scatter(jj, c):
            dst = tile_at(j0 + jj)
            o_ref[0, dst, :] += val_ref[0, 0, j0 + jj] * yb_ref[tile_of(jj), :]
            return c

        def scatter_group(gi, c):
            j = gi * ROW_UNROLL
            dsts = [tile_at(j0 + j + u) for u in range(ROW_UNROLL)]
            rows = [o_ref[0, dsts[u], :] + val_ref[0, 0, j0 + j + u] * yb_ref[tile_of(j + u), :]
                    for u in range(ROW_UNROLL)]
            for u in range(ROW_UNROLL):
                o_ref[0, dsts[u], :] = rows[u]
            return c

        lax.fori_loop(0, ngroup, scatter_group, 0)
        lax.fori_loop(ngroup * ROW_UNROLL, cnt, scatter, 0)
        return carry

    lax.fori_loop(0, (hi - lo + slots - 1) // slots, chunk, 0)


def _moe(h2, idx, vals, off, cap, wg, wu, wd, layer, rng):
    _, ne, d, f = wg.shape
    nchunk = d // LANES
    b, t = h2.shape[0], h2.shape[1] // nchunk
    nrange = t // rng
    capp = idx.shape[-1]
    slots = min(MOE_SLOTS, -(-cap // 16) * 16)
    lo = off[:, :, ::rng // LANES, 0][:, :, :nrange]
    lo = jnp.concatenate([lo, jnp.full((b, ne, 1), cap, jnp.int32)], axis=-1).reshape(-1)
    lst = pl.BlockSpec((1, 1, capp), lambda bi, r, e, lo_ref: (bi * ne + e, 0, 0), memory_space=pltpu.SMEM)
    tile = pl.BlockSpec((1, rng * nchunk, LANES), lambda bi, r, e, lo_ref: (bi, r, 0))
    return pl.pallas_call(
        functools.partial(_moe_kernel, nrange, nchunk, slots), name="moe",
        grid_spec=pltpu.PrefetchScalarGridSpec(
            num_scalar_prefetch=1,
            grid=(b, nrange, ne),
            in_specs=[lst, lst, tile,
                      pl.BlockSpec((None, 1, d, f), lambda bi, r, e, lo_ref: (layer, e, 0, 0)),
                      pl.BlockSpec((None, 1, d, f), lambda bi, r, e, lo_ref: (layer, e, 0, 0)),
                      pl.BlockSpec((None, 1, f, d), lambda bi, r, e, lo_ref: (layer, e, 0, 0))],
            out_specs=tile,
            scratch_shapes=[pltpu.VMEM((slots * nchunk, LANES), F32)] * 2),
        out_shape=jax.ShapeDtypeStruct(h2.shape, F32),
        compiler_params=_cparams(("arbitrary", "arbitrary", "arbitrary")),
    )(lo, idx.reshape(b * ne, 1, capp), vals.reshape(b * ne, 1, capp), h2, wg, wu, wd)


def _residual_kernel(x_ref, y_ref, g_ref, o_ref):
    o_ref[0] = x_ref[0] + g_ref[0] * _load_row_tiles(y_ref, (0,), x_ref.shape[1])


def _residual(x, y, gate, tm):
    b, t, d = x.shape
    row = pl.BlockSpec((1, tm, d), lambda bi, i: (bi, i, 0))
    return pl.pallas_call(
        _residual_kernel, name="residual", grid=(b, t // tm),
        in_specs=[row, pl.BlockSpec((1, tm * d // LANES, LANES), lambda bi, i: (bi, i, 0)),
                  pl.BlockSpec((1, 1, d), lambda bi, i: (bi, 0, 0))],
        out_specs=row, out_shape=jax.ShapeDtypeStruct((b, t, d), F32),
        compiler_params=_cparams(("arbitrary", "arbitrary")),
    )(x, y, gate)


def _hgrn_lower_bounds(logits):
    p = jax.nn.softmax(logits.astype(F32), axis=1)
    return jnp.concatenate([jnp.zeros_like(p[:, :1]), jnp.cumsum(p[:, 1:], axis=1)], axis=1)


def _expert_choice_ffn(h2, aff, wg, wu, wd, layer, rng):
    b, ne, t = aff.shape
    cap = EC_CAPACITY_FACTOR * t // ne
    rows = max(t // LANES, 8)
    pad = rows * LANES - t
    if pad:
        aff = jnp.pad(aff, ((0, 0), (0, 0), (0, pad)), constant_values=-1.0)
    idx, vals, off = _route(aff.reshape(b, ne, rows, LANES), cap, rng, h2.shape[1] // t)
    return _moe(h2, idx, vals, off, cap, wg, wu, wd, layer, rng)


def _tiles(t, ctx_len):
    rows = t // GRID_W
    assert t % GRID_W == 0 and rows % NA_QROWS == 0 and rows >= NA_KROWS, "latent grid too small for the window tiles"
    assert ctx_len % HG_CHUNK == 0 and (ctx_len // HG_CHUNK) % GLA_GROUP == 0 and ctx_len % LANES == 0
    proj = min(PROJ_ROWS, t)
    scan = min(SCAN_ROWS, t)
    rng = min(MOE_RANGE, t)
    assert t % proj == 0 and t % scan == 0 and t % rng == 0 and (scan // HG_CHUNK) % GLA_GROUP == 0
    return proj, scan, rng


def kernel(x, c, ctx, c_ctx, w_mod, b_mod, norm1_w, w_in, na_q_norm, na_k_norm, na_rpb, hg_lb_logits, hg_norm,
           conv_w, w_out, norm2_w, w_router, w_exp_gate, w_exp_up, w_exp_down):
    b, t, d = x.shape
    wg, wu, wd = w_exp_gate.astype(BF16), w_exp_up.astype(BF16), w_exp_down.astype(BF16)
    depth = w_mod.shape[0]
    ctx_len = ctx.shape[1]
    assert b + 1 <= 8 and d % LANES == 0
    proj_rows, scan_rows, moe_range = _tiles(t, ctx_len)
    cond = jnp.zeros((8, d), F32).at[0:b].set(c).at[b].set(c_ctx)
    mod = _modvec(cond, w_mod, b_mod).reshape(depth, 8, N_MOD, d)
    cos, sa, sb = _rope_tables(t)
    cos_c = jnp.ones((ctx_len, LANES), F32)
    zero_c = jnp.zeros((ctx_len, LANES), F32)
    lb = _hgrn_lower_bounds(hg_lb_logits)
    w_in_p = _permute_w_in(w_in)
    w_out_b = w_out.astype(BF16)
    w_router_t = w_router.transpose(0, 2, 1).astype(BF16)
    bias = jax.vmap(lambda r: _na_bias_tiles(r, t // GRID_W))(na_rpb)
    tile2 = lambda v: jnp.tile(v, LANES // HEAD_DIM)[None]

    cx = ctx
    y = y_c = None
    for l in range(depth):
        need_ctx = l < depth - 1
        m_lat = [mod[l, 0:b, j][:, None, :] for j in range(N_MOD)]
        m_ctx = [jnp.broadcast_to(mod[l, b, j][None, None, :], (b, 1, d)) for j in range(N_MOD)]
        g_prev = [mod[l - 1, 0:b, 5][:, None, :], jnp.broadcast_to(mod[l - 1, b, 5][None, None, :], (b, 1, d))] if l else [None, None]
        qn, kn = tile2(na_q_norm[l]), tile2(na_k_norm[l])
        x, qkv, rest = _inproj(x, y, g_prev[0], m_lat[0], m_lat[1], norm1_w[l][None], w_in_p, l, cos, sa, sb, qn, kn,
                                 proj_rows)
        cx, qkv_c, rest_c = _inproj(cx, y_c, g_prev[1], m_ctx[0], m_ctx[1], norm1_w[l][None], w_in_p, l,
                                    cos_c, zero_c, zero_c, qn, kn, ctx_len)
        na = _na_attention(qkv, qkv_c, bias, l)
        o = o_c = None
        for rev in (False, True):
            lbd = lb[1 if rev else 0, l][None]
            o, o_c = _gla_scan(rest, rest_c, jnp.log(lbd), jnp.log1p(-lbd), rev, o, o_c,
                               jnp.tile(hg_norm[l], HG_HEADS)[None] if rev else None, tb=scan_rows)
        x, h2, aff = _outproj(na, o, rest, x, w_out_b, l, m_lat[2], m_lat[3], m_lat[4], norm2_w[l][None],
                              conv_w[l], w_router_t[l], proj_rows)
        y = _expert_choice_ffn(h2, aff, wg, wu, wd, l, moe_range)
        if need_ctx:
            na_c = _ctx_attention(qkv_c)
            cx, h2_c, aff_c = _outproj(na_c, o_c, rest_c, cx, w_out_b, l, m_ctx[2], m_ctx[3], m_ctx[4],
                                       norm2_w[l][None], conv_w[l], w_router_t[l], ctx_len)
            y_c = _expert_choice_ffn(h2_c, aff_c, wg, wu, wd, l, ctx_len)
    return _residual(x, y, mod[depth - 1, 0:b, 5][:, None, :], proj_rows)
```
